```python
import math
import jax, jax.numpy as jnp
from jax import lax
import numpy as np

D_MODEL = 4096
BATCH = 4
SEQ = 2048
DEPTH = 4

CTX_LEN = 256
GRID_W = 64
N_MIXERS = 3
HEAD_DIM = 128
ROPE_THETA = 10000.0
NORM_EPS = 1e-6
MASK_VALUE = -1e30
ADA_RANK = D_MODEL // 8
A_HEADS = D_MODEL // HEAD_DIM
A_KV_HEADS = A_HEADS // 4
WINDOW = 128
BLOCK = 128
B_HEADS = D_MODEL // (2 * HEAD_DIM)
CONV_W = 3
N_EXPERTS = 64
TOP_K = 8
N_GROUPS = 8
TOPK_GROUPS = 4
EXPERT_FF = D_MODEL * 3 // 64
SHARED_FF = EXPERT_FF
ROUTED_SCALE = 2.5

kernel_name = "hybrid_interleaved_dit_moe_prefix"


def rms_norm(x, g):
    xf = x.astype(jnp.float32)
    y = xf * lax.rsqrt(jnp.mean(xf * xf, axis=-1, keepdims=True) + NORM_EPS)
    return (y * g.astype(jnp.float32)).astype(x.dtype)


def modulate(h, shift, scale):
    return h * (1 + scale) + shift


def axial_rope_tables(n_tok):
    rows = n_tok // GRID_W
    row = jnp.broadcast_to(jnp.arange(rows)[:, None], (rows, GRID_W)).reshape(-1).astype(jnp.float32)
    col = jnp.broadcast_to(jnp.arange(GRID_W)[None, :], (rows, GRID_W)).reshape(-1).astype(jnp.float32)
    n_freq = HEAD_DIM // 4
    inv = ROPE_THETA ** (-jnp.arange(n_freq, dtype=jnp.float32) / n_freq)
    ang = jnp.concatenate([row[:, None] * inv, col[:, None] * inv], axis=-1)
    return jnp.cos(ang), jnp.sin(ang)


def apply_rope(x, cos, sin):
    half = HEAD_DIM // 2
    shape = (1, x.shape[1]) + (1,) * (x.ndim - 3) + (half,)
    cs = cos.reshape(shape).astype(x.dtype)
    sn = sin.reshape(shape).astype(x.dtype)
    x1, x2 = x[..., :half], x[..., half:]
    return jnp.concatenate([x1 * cs - x2 * sn, x2 * cs + x1 * sn], axis=-1)


def _bands(t, nb):
    bsz = t.shape[0]
    tp = jnp.pad(t, ((0, 0), (BLOCK, BLOCK), (0, 0), (0, 0))).reshape(bsz, nb + 2, BLOCK, t.shape[2], t.shape[3])
    return jnp.concatenate([tp[:, :-2], tp[:, 1:-1], tp[:, 2:]], axis=2)


def windowed_gqa_sink(h, hc, w_qkv, w_o, sink, cos, sin, need_ctx):
    bsz, n_tok, _ = h.shape
    n_ctx = hc.shape[1]
    kv, grp, dh = A_KV_HEADS, A_HEADS // A_KV_HEADS, HEAD_DIM
    nq, nk = A_HEADS * dh, kv * dh
    scale = dh ** -0.5
    qkv = h @ w_qkv
    q = apply_rope(qkv[..., :nq].reshape(bsz, n_tok, kv, grp, dh), cos, sin)
    k = apply_rope(qkv[..., nq:nq + nk].reshape(bsz, n_tok, kv, dh), cos, sin)
    v = qkv[..., nq + nk:].reshape(bsz, n_tok, kv, dh)
    qkv_c = hc @ (w_qkv if need_ctx else w_qkv[:, nq:])
    kc = qkv_c[..., -2 * nk:-nk].reshape(bsz, n_ctx, kv, dh)
    vc = qkv_c[..., -nk:].reshape(bsz, n_ctx, kv, dh)
    nb = n_tok // BLOCK
    qb = q.reshape(bsz, nb, BLOCK, kv, grp, dh)
    kb, vb = _bands(k, nb), _bands(v, nb)
    s_lat = jnp.einsum('bnqkgd,bnskd->bnkgqs', qb, kb).astype(jnp.float32) * scale
    s_ctx = jnp.einsum('bnqkgd,bckd->bnkgqc', qb, kc).astype(jnp.float32) * scale
    qpos = jnp.arange(nb)[:, None] * BLOCK + jnp.arange(BLOCK)[None, :]
    kpos = jnp.arange(nb)[:, None] * BLOCK - BLOCK + jnp.arange(3 * BLOCK)[None, :]
    rel = kpos[:, None, :] - qpos[:, :, None]
    valid = (jnp.abs(rel) <= WINDOW) & (kpos[:, None, :] >= 0) & (kpos[:, None, :] < n_tok)
    s_lat = jnp.where(valid[None, :, None, None], s_lat, MASK_VALUE)
    sink_col = jnp.broadcast_to(sink.astype(jnp.float32).reshape(1, 1, kv, grp, 1, 1), s_lat.shape[:-1] + (1,))
    p = jax.nn.softmax(jnp.concatenate([s_lat, s_ctx, sink_col], axis=-1), axis=-1).astype(v.dtype)
    nw = 3 * BLOCK
    o = (jnp.einsum('bnkgqs,bnskd->bnqkgd', p[..., :nw], vb)
         + jnp.einsum('bnkgqc,bckd->bnqkgd', p[..., nw:nw + n_ctx], vc))
    y = o.reshape(bsz, n_tok, nq) @ w_o
    if not need_ctx:
        return y, None
    qc = qkv_c[..., :nq].reshape(bsz, n_ctx, kv, grp, dh)
    sc = jnp.einsum('bqkgd,bckd->bkgqc', qc, kc).astype(jnp.float32) * scale
    sink_c = jnp.broadcast_to(sink.astype(jnp.float32).reshape(1, kv, grp, 1, 1), sc.shape[:-1] + (1,))
    pc = jax.nn.softmax(jnp.concatenate([sc, sink_c], axis=-1), axis=-1).astype(vc.dtype)
    oc = jnp.einsum('bkgqc,bckd->bqkgd', pc[..., :n_ctx], vc)
    yc = oc.reshape(bsz, n_ctx, nq) @ w_o
    return y, yc


def _diff_core(q, k, v, lam, scale):
    s = jnp.einsum('bqhcd,bkhcd->bhcqk', q, k).astype(jnp.float32) * scale
    p = jax.nn.softmax(s, axis=-1)
    a = (p[:, :, 0] - lam * p[:, :, 1]).astype(v.dtype)
    return jnp.einsum('bhqk,bkhe->bqhe', a, v)


def differential_attention(h, hc, w_qkv, w_o, lam_vecs, subln_g, lam_init, cos, sin, need_ctx):
    bsz, n_tok, _ = h.shape
    n_ctx = hc.shape[1]
    nh, dh = B_HEADS, HEAD_DIM
    nq = nh * 2 * dh
    scale = dh ** -0.5
    qkv = h @ w_qkv
    q = apply_rope(qkv[..., :nq].reshape(bsz, n_tok, nh, 2, dh), cos, sin)
    k = apply_rope(qkv[..., nq:2 * nq].reshape(bsz, n_tok, nh, 2, dh), cos, sin)
    v = qkv[..., 2 * nq:].reshape(bsz, n_tok, nh, 2 * dh)
    qkv_c = hc @ (w_qkv if need_ctx else w_qkv[:, nq:])
    kc = qkv_c[..., -2 * nq:-nq].reshape(bsz, n_ctx, nh, 2, dh)
    vc = qkv_c[..., -nq:].reshape(bsz, n_ctx, nh, 2 * dh)
    lf = lam_vecs.astype(jnp.float32)
    lam = jnp.exp(jnp.sum(lf[0] * lf[1])) - jnp.exp(jnp.sum(lf[2] * lf[3])) + lam_init
    k_all = jnp.concatenate([k, kc], axis=1)
    v_all = jnp.concatenate([v, vc], axis=1)
    nb = n_tok // BLOCK
    qb = jnp.moveaxis(q.reshape(bsz, nb, BLOCK, nh, 2, dh), 1, 0)
    o = lax.map(lambda q_blk: _diff_core(q_blk, k_all, v_all, lam, scale), qb)
    o = jnp.moveaxis(o, 0, 1).reshape(bsz, n_tok, nh, 2 * dh)
    out_scale = 1.0 - lam_init
    y = (rms_norm(o, subln_g) * out_scale).reshape(bsz, n_tok, nq) @ w_o
    if not need_ctx:
        return y, None
    qc = qkv_c[..., :nq].reshape(bsz, n_ctx, nh, 2, dh)
    oc = _diff_core(qc, kc, vc, lam, scale)
    yc = (rms_norm(oc, subln_g) * out_scale).reshape(bsz, n_ctx, nq) @ w_o
    return y, yc


def depthwise_conv_centred(u, w):
    return lax.conv_general_dilated(
        u, w[:, None, :].astype(u.dtype), window_strides=(1,),
        padding=[(CONV_W // 2, CONV_W // 2)],
        dimension_numbers=('NWC', 'WIO', 'NWC'), feature_group_count=u.shape[-1])


def short_conv_mixer(h, w_in, conv_w, w_out):
    gate_b, gate_c, u = jnp.split(h @ w_in, 3, axis=-1)
    z = depthwise_conv_centred(gate_c * u, conv_w)
    return (gate_b * z) @ w_out


def moe_ffn(h, router_w, router_b, w_gate, w_up, w_down, s_gate, s_up, s_down):
    shp = h.shape
    t = h.reshape(-1, shp[-1])
    scores = jax.nn.sigmoid((t @ router_w).astype(jnp.float32))
    biased = scores + router_b.astype(jnp.float32)
    per_group = N_EXPERTS // N_GROUPS
    grp_score = lax.top_k(biased.reshape(-1, N_GROUPS, per_group), 2)[0].sum(-1)
    _, gidx = lax.top_k(grp_score, TOPK_GROUPS)
    gmask = jnp.any(gidx[..., None] == jnp.arange(N_GROUPS), axis=-2)
    emask = jnp.repeat(gmask, per_group, axis=-1)
    _, eidx = lax.top_k(jnp.where(emask, biased, MASK_VALUE), TOP_K)
    w = jnp.take_along_axis(scores, eidx, axis=-1)
    w = w / jnp.sum(w, axis=-1, keepdims=True) * ROUTED_SCALE
    gates = jnp.sum(jnp.where(eidx[..., None] == jnp.arange(N_EXPERTS), w[..., None], 0.0), axis=-2).astype(t.dtype)
    hg = jnp.einsum('td,edf->tef', t, w_gate)
    hu = jnp.einsum('td,edf->tef', t, w_up)
    act = jax.nn.silu(hg) * hu * gates[..., None]
    routed = jnp.einsum('tef,efd->td', act, w_down)
    shared = (jax.nn.silu(t @ s_gate) * (t @ s_up)) @ s_down
    return (routed + shared).reshape(shp)


def setup_inputs(seed: int = 0) -> dict:
    key = jax.random.key(seed)
    ks = jax.random.split(key, 28)
    d = D_MODEL
    n_a = (DEPTH + N_MIXERS - 1) // N_MIXERS
    n_b = (DEPTH + N_MIXERS - 2) // N_MIXERS
    n_c = DEPTH // N_MIXERS
    f32 = jnp.float32

    def nrm(k, shape, s):
        return jax.random.normal(k, shape, f32) * s

    qkv_a = (A_HEADS + 2 * A_KV_HEADS) * HEAD_DIM
    wb = B_HEADS * 2 * HEAD_DIM
    return {
        'x': nrm(ks[0], (BATCH, SEQ, d), 1.0),
        'c': nrm(ks[1], (BATCH, d), 1.0),
        'ctx': nrm(ks[2], (BATCH, CTX_LEN, d), 1.0),
        'c_ctx': nrm(ks[3], (d,), 1.0),
        'ada_down': nrm(ks[4], (DEPTH, d, ADA_RANK), d ** -0.5),
        'ada_up': nrm(ks[5], (DEPTH, ADA_RANK, 6 * d), 0.3 * ADA_RANK ** -0.5),
        'ada_b': nrm(ks[6], (DEPTH, 6 * d), 0.02),
        'norm1_g': 1.0 + nrm(ks[7], (DEPTH, d), 0.02),
        'norm2_g': 1.0 + nrm(ks[8], (DEPTH, d), 0.02),
        'a_w_qkv': nrm(ks[9], (n_a, d, qkv_a), d ** -0.5),
        'a_w_o': nrm(ks[10], (n_a, A_HEADS * HEAD_DIM, d), (A_HEADS * HEAD_DIM) ** -0.5),
        'a_sink': nrm(ks[11], (n_a, A_HEADS), 0.5),
        'b_w_qkv': nrm(ks[12], (n_b, d, 3 * wb), d ** -0.5),
        'b_w_o': nrm(ks[13], (n_b, wb, d), wb ** -0.5),
        'b_lam': nrm(ks[14], (n_b, 4, HEAD_DIM), 0.1),
        'b_subln_g': 1.0 + nrm(ks[15], (n_b, 2 * HEAD_DIM), 0.02),
        'c_w_in': nrm(ks[16], (n_c, d, 3 * d), d ** -0.5),
        'c_conv': nrm(ks[17], (n_c, CONV_W, d), CONV_W ** -0.5),
        'c_w_out': nrm(ks[18], (n_c, d, d), d ** -0.5),
        'router_w': nrm(ks[19], (DEPTH, d, N_EXPERTS), d ** -0.5),
        'router_b': nrm(ks[20], (DEPTH, N_EXPERTS), 0.01),
        'exp_gate': nrm(ks[21], (DEPTH, N_EXPERTS, d, EXPERT_FF), d ** -0.5),
        'exp_up': nrm(ks[22], (DEPTH, N_EXPERTS, d, EXPERT_FF), d ** -0.5),
        'exp_down': nrm(ks[23], (DEPTH, N_EXPERTS, EXPERT_FF, d), EXPERT_FF ** -0.5),
        'sh_gate': nrm(ks[24], (DEPTH, d, SHARED_FF), d ** -0.5),
        'sh_up': nrm(ks[25], (DEPTH, d, SHARED_FF), d ** -0.5),
        'sh_down': nrm(ks[26], (DEPTH, SHARED_FF, d), SHARED_FF ** -0.5),
        'final_g': 1.0 + nrm(ks[27], (d,), 0.02),
    }


def reference(x, c, ctx, c_ctx, ada_down, ada_up, ada_b, norm1_g, norm2_g,
              a_w_qkv, a_w_o, a_sink, b_w_qkv, b_w_o, b_lam, b_subln_g,
              c_w_in, c_conv, c_w_out, router_w, router_b, exp_gate, exp_up, exp_down,
              sh_gate, sh_up, sh_down, final_g):
    cos, sin = axial_rope_tables(x.shape[1])
    mod_src = jax.nn.silu(c)
    mod_src_ctx = jax.nn.silu(c_ctx)
    for i in range(DEPTH):
        kind, j = i % N_MIXERS, i // N_MIXERS
        need_ctx = i < DEPTH - 1
        mod = (mod_src @ ada_down[i]) @ ada_up[i] + ada_b[i]
        sh1, sc1, g1, sh2, sc2, g2 = jnp.split(mod[:, None, :], 6, axis=-1)
        modc = (mod_src_ctx @ ada_down[i]) @ ada_up[i] + ada_b[i]
        csh1, csc1, cg1, csh2, csc2, cg2 = jnp.split(modc, 6, axis=-1)
        a = modulate(rms_norm(x, norm1_g[i]), sh1, sc1)
        if kind == 0:
            ac = modulate(rms_norm(ctx, norm1_g[i]), csh1, csc1)
            y, yc = windowed_gqa_sink(a, ac, a_w_qkv[j], a_w_o[j], a_sink[j], cos, sin, need_ctx)
        elif kind == 1:
            ac = modulate(rms_norm(ctx, norm1_g[i]), csh1, csc1)
            lam_init = 0.8 - 0.6 * math.exp(-0.3 * i)
            y, yc = differential_attention(a, ac, b_w_qkv[j], b_w_o[j], b_lam[j], b_subln_g[j],
                                           lam_init, cos, sin, need_ctx)
        else:
            y = short_conv_mixer(a, c_w_in[j], c_conv[j], c_w_out[j])
            yc = None
            if need_ctx:
                ac = modulate(rms_norm(ctx, norm1_g[i]), csh1, csc1)
                yc = short_conv_mixer(ac, c_w_in[j], c_conv[j], c_w_out[j])
        x = x + g1 * y
        f = modulate(rms_norm(x, norm2_g[i]), sh2, sc2)
        x = x + g2 * moe_ffn(f, router_w[i], router_b[i], exp_gate[i], exp_up[i], exp_down[i],
                             sh_gate[i], sh_up[i], sh_down[i])
        if need_ctx:
            ctx = ctx + cg1 * yc
            fc = modulate(rms_norm(ctx, norm2_g[i]), csh2, csc2)
            ctx = ctx + cg2 * moe_ffn(fc, router_w[i], router_b[i], exp_gate[i], exp_up[i], exp_down[i],
                                      sh_gate[i], sh_up[i], sh_down[i])
    return rms_norm(x, final_g)
```

```python
import functools
import math

import jax
import jax.numpy as jnp
from jax import lax
from jax.experimental import pallas as pl
from jax.experimental.pallas import tpu as pltpu

F32 = jnp.float32
BF16 = jnp.bfloat16
U32 = jnp.uint32
I32 = jnp.int32

HEAD_DIM = 128
GRID_W = 64
ROPE_THETA = 10000.0
NORM_EPS = 1e-6
MASK_VALUE = -1e30
WINDOW = 128
BLOCK = 128
CONV_W = 3
N_MIXERS = 3
N_EXPERTS = 64
TOP_K = 8
N_GROUPS = 8
TOPK_GROUPS = 4
ROUTED_SCALE = 2.5
KV_GROUP = 4
MOD_ROWS = 8

LANE = 128
VMEM_LIMIT_BYTES = 56 * 1024 * 1024
GMM_TILE = 256
HI_MASK = 0xFFFF0000


def _params(*sem):
    return pltpu.CompilerParams(dimension_semantics=sem, vmem_limit_bytes=VMEM_LIMIT_BYTES)


def _tile(n, pref):
    t = min(n, pref)
    while n % t:
        t //= 2
    return t


def _silu(v):
    return v * jax.nn.sigmoid(v)


def _pack_bf16_pair(lo, hi):
    lo_b = lax.bitcast_convert_type(lo.astype(BF16).astype(F32), U32)
    hi_b = lax.bitcast_convert_type(hi.astype(BF16).astype(F32), U32)
    return (lo_b >> 16) | (hi_b & jnp.uint32(HI_MASK))


def _unpack_bf16_pair(u):
    lo = lax.bitcast_convert_type(u << 16, F32)
    hi = lax.bitcast_convert_type(u & jnp.uint32(HI_MASK), F32)
    return lo, hi


def _ada_kernel(src_ref, down_ref, up_ref, b_ref, out_ref, t1_ref):
    @pl.when(pl.program_id(1) == 0)
    def _():
        s = _silu(src_ref[...]).astype(BF16)
        t1 = jnp.dot(s, down_ref[...].astype(BF16), preferred_element_type=F32)
        t1_ref[...] = t1.astype(BF16)

    out_ref[...] = jnp.dot(t1_ref[...], up_ref[...].astype(BF16),
                           preferred_element_type=F32) + b_ref[...]


def _ada_table(src, ada_down, ada_up, ada_b):
    depth, d, rank = ada_down.shape
    n6 = ada_up.shape[2]
    tn = _tile(n6, 2048)
    return pl.pallas_call(
        _ada_kernel,
        grid=(depth, n6 // tn),
        in_specs=[
            pl.BlockSpec((MOD_ROWS, d), lambda l, j: (0, 0)),
            pl.BlockSpec((None, d, rank), lambda l, j: (l, 0, 0)),
            pl.BlockSpec((None, rank, tn), lambda l, j: (l, 0, j)),
            pl.BlockSpec((None, 1, tn), lambda l, j: (l, 0, j)),
        ],
        out_specs=pl.BlockSpec((None, MOD_ROWS, tn), lambda l, j: (l, 0, j)),
        out_shape=jax.ShapeDtypeStruct((depth, MOD_ROWS, n6), F32),
        scratch_shapes=[pltpu.VMEM((MOD_ROWS, rank), BF16)],
        compiler_params=_params("arbitrary", "arbitrary"),
        name="ada_table",
    )(src, ada_down, ada_up, ada_b.reshape(depth, 1, n6))


def _norm_mod(xf, g, sc, sh):
    y = xf * lax.rsqrt(jnp.mean(xf * xf, axis=-1, keepdims=True) + NORM_EPS)
    return (y * g) * (1.0 + sc) + sh


class _Cfg:
    def __init__(self, batch, seq, ctx_len, d):
        self.batch, self.seq, self.ctx_len, self.d = batch, seq, ctx_len, d
        self.t_lat = batch * seq
        self.t_ctx = batch * ctx_len
        self.t = self.t_lat + self.t_ctx

    def group_of_rows(self, rows_per_block):
        per_seq = self.seq // rows_per_block
        assert per_seq * rows_per_block == self.seq
        assert self.t_lat % rows_per_block == 0 and self.t_ctx % rows_per_block == 0
        batch = self.batch
        return lambda i: jnp.minimum(i // per_seq, batch)


def _mod_spec(cfg, layer, chunk, rows_per_block, width, col_of=None):
    grp = cfg.group_of_rows(rows_per_block)
    per_chunk = cfg.d // width
    if col_of is None:
        return pl.BlockSpec((None, 1, width),
                            lambda i: (layer * MOD_ROWS + grp(i), 0, chunk * per_chunk))
    return pl.BlockSpec((None, 1, width),
                        lambda j, i: (layer * MOD_ROWS + grp(i), 0, chunk * per_chunk + j))


def _norm1_kernel(x_ref, g_ref, sc_ref, sh_ref, a_ref):
    a_ref[...] = _norm_mod(x_ref[...], g_ref[...], sc_ref[...], sh_ref[...]).astype(BF16)


def _norm1(cfg, xs, norm_g, mod, layer):
    t, d = xs.shape
    rb = _tile(cfg.ctx_len, 256)
    return pl.pallas_call(
        _norm1_kernel,
        grid=(t // rb,),
        in_specs=[
            pl.BlockSpec((rb, d), lambda i: (i, 0)),
            pl.BlockSpec((None, 1, d), lambda i: (layer, 0, 0)),
            _mod_spec(cfg, layer, 1, rb, d),
            _mod_spec(cfg, layer, 0, rb, d),
        ],
        out_specs=pl.BlockSpec((rb, d), lambda i: (i, 0)),
        out_shape=jax.ShapeDtypeStruct((t, d), BF16),
        compiler_params=_params("parallel"),
        name="norm1",
    )(xs, norm_g, mod, mod)


def _norm2_kernel(x_ref, g_ref, sc_ref, sh_ref, rw_ref, sg_ref, su_ref,
                  fp_ref, logit_ref, hsh_ref):
    f = _norm_mod(x_ref[...], g_ref[...], sc_ref[...], sh_ref[...])
    half = f.shape[1] // 2
    fp_ref[...] = _pack_bf16_pair(f[:, :half], f[:, half:])
    logit_ref[...] = jnp.dot(f, rw_ref[...], preferred_element_type=F32,
                             precision=lax.Precision.HIGHEST)
    fb = f.astype(BF16)
    hg = jnp.dot(fb, sg_ref[...], preferred_element_type=F32)
    hu = jnp.dot(fb, su_ref[...], preferred_element_type=F32)
    hsh_ref[...] = (_silu(hg) * hu).astype(BF16)


def _norm2(cfg, xs, norm_g, mod, layer, router_w, sh_gate_bf, sh_up_bf):
    t, d = xs.shape
    ne = router_w.shape[2]
    ff = sh_gate_bf.shape[2]
    rb = _tile(cfg.ctx_len, 256)
    return pl.pallas_call(
        _norm2_kernel,
        grid=(t // rb,),
        in_specs=[
            pl.BlockSpec((rb, d), lambda i: (i, 0)),
            pl.BlockSpec((None, 1, d), lambda i: (layer, 0, 0)),
            _mod_spec(cfg, layer, 4, rb, d),
            _mod_spec(cfg, layer, 3, rb, d),
            pl.BlockSpec((None, d, ne), lambda i: (layer, 0, 0)),
            pl.BlockSpec((None, d, ff), lambda i: (layer, 0, 0)),
            pl.BlockSpec((None, d, ff), lambda i: (layer, 0, 0)),
        ],
        out_specs=[
            pl.BlockSpec((rb, d // 2), lambda i: (i, 0)),
            pl.BlockSpec((rb, ne), lambda i: (i, 0)),
            pl.BlockSpec((rb, ff), lambda i: (i, 0)),
        ],
        out_shape=[
            jax.ShapeDtypeStruct((t, d // 2), U32),
            jax.ShapeDtypeStruct((t, ne), F32),
            jax.ShapeDtypeStruct((t, ff), BF16),
        ],
        compiler_params=_params("parallel"),
        name="norm2_router",
    )(xs, norm_g, mod, mod, router_w, sh_gate_bf, sh_up_bf)


def _cast_weight(w_ref, wb_ref):
    @pl.when(pl.program_id(1) == 0)
    def _():
        wb_ref[...] = w_ref[...].astype(BF16)


def _proj_kernel(x_ref, w_ref, o_ref, wb_ref):
    _cast_weight(w_ref, wb_ref)
    o_ref[...] = jnp.dot(x_ref[...], wb_ref[...], preferred_element_type=F32).astype(o_ref.dtype)


def _proj_rope_kernel(x_ref, w_ref, cos_ref, sin_ref, o_ref, wb_ref, *, rope_cols, rope_rows):
    _cast_weight(w_ref, wb_ref)
    acc = jnp.dot(x_ref[...], wb_ref[...], preferred_element_type=F32)
    do_rope = jnp.logical_and(pl.program_id(0) < rope_cols, pl.program_id(1) < rope_rows)

    @pl.when(do_rope)
    def _():
        c, s = cos_ref[...], sin_ref[...]
        for h in range(acc.shape[1] // HEAD_DIM):
            sl = slice(h * HEAD_DIM, (h + 1) * HEAD_DIM)
            xh = acc[:, sl]
            o_ref[:, sl] = (xh * c + pltpu.roll(xh, HEAD_DIM // 2, 1) * s).astype(o_ref.dtype)

    @pl.when(jnp.logical_not(do_rope))
    def _():
        o_ref[...] = acc.astype(o_ref.dtype)


def _proj_resid_kernel(x_ref, w_ref, res_ref, gate_ref, o_ref, wb_ref):
    _cast_weight(w_ref, wb_ref)
    acc = jnp.dot(x_ref[...], wb_ref[...], preferred_element_type=F32)
    o_ref[...] = res_ref[...] + gate_ref[...] * acc


def _proj_tiles(m, n, seq, n_align=0):
    return _tile(math.gcd(m, seq), 1024), _tile(math.gcd(n, n_align), 512)


def _proj(cfg, x, w, layer_j, rope=None):
    m, k = x.shape
    n = w.shape[2]
    tm, tn = _proj_tiles(m, n, cfg.seq, 0 if rope is None else rope[2])
    grid = (n // tn, m // tm)
    x_spec = pl.BlockSpec((tm, k), lambda j, i: (i, 0))
    w_spec = pl.BlockSpec((None, k, tn), lambda j, i: (layer_j, 0, j))
    o_spec = pl.BlockSpec((tm, tn), lambda j, i: (i, j))
    common = dict(
        grid=grid, out_specs=o_spec,
        out_shape=jax.ShapeDtypeStruct((m, n), BF16),
        scratch_shapes=[pltpu.VMEM((k, tn), BF16)],
        compiler_params=_params("arbitrary", "arbitrary"),
    )
    if rope is None:
        return pl.pallas_call(_proj_kernel, in_specs=[x_spec, w_spec], name="proj", **common)(x, w)
    cos_t, sin_t, rope_n = rope
    per_seq = cfg.seq // tm
    t_spec = pl.BlockSpec((tm, HEAD_DIM), lambda j, i: (i % per_seq, 0))
    kern = functools.partial(_proj_rope_kernel, rope_cols=rope_n // tn, rope_rows=cfg.t_lat // tm)
    return pl.pallas_call(kern, in_specs=[x_spec, w_spec, t_spec, t_spec],
                          name="proj_rope", **common)(x, w, cos_t, sin_t)


def _proj_resid(cfg, x, w, layer_j, res, mod, layer, gate_chunk):
    m, k = x.shape
    n = w.shape[2]
    tm, tn = _proj_tiles(m, n, cfg.seq)
    return pl.pallas_call(
        _proj_resid_kernel,
        grid=(n // tn, m // tm),
        in_specs=[
            pl.BlockSpec((tm, k), lambda j, i: (i, 0)),
            pl.BlockSpec((None, k, tn), lambda j, i: (layer_j, 0, j)),
            pl.BlockSpec((tm, tn), lambda j, i: (i, j)),
            _mod_spec(cfg, layer, gate_chunk, tm, tn, col_of=True),
        ],
        out_specs=pl.BlockSpec((tm, tn), lambda j, i: (i, j)),
        out_shape=jax.ShapeDtypeStruct((m, n), F32),
        scratch_shapes=[pltpu.VMEM((k, tn), BF16)],
        compiler_params=_params("arbitrary", "arbitrary"),
        name="proj_resid",
    )(x, w, res, mod)


_NT = (((1,), (1,)), ((), ()))


def _gqa_head_group(q_ref, k_parts, v_parts, valid, sink_ref, layer_j, kvh, o_ref):
    dh = HEAD_DIM
    rows = q_ref.shape[0]
    scale = dh ** -0.5
    heads = [kvh * KV_GROUP + g for g in range(KV_GROUP)]
    qg = jnp.concatenate([q_ref[:, h * dh:(h + 1) * dh] for h in heads], axis=0)
    scores = []
    for idx, kp in enumerate(k_parts):
        s = lax.dot_general(qg, kp, _NT, preferred_element_type=F32) * scale
        if idx == 0 and valid is not None:
            s = jnp.where(valid, s, MASK_VALUE)
        scores.append(s)
    row_head = lax.broadcasted_iota(I32, (KV_GROUP * rows, 1), 0) // rows
    sink_col = jnp.zeros((KV_GROUP * rows, 1), F32)
    for g, h in enumerate(heads):
        sink_col = jnp.where(row_head == g, sink_ref[layer_j, h], sink_col)
    m = sink_col
    for s in scores:
        m = jnp.maximum(m, jnp.max(s, axis=-1, keepdims=True))
    es = [jnp.exp(s - m) for s in scores]
    denom = jnp.exp(sink_col - m)
    for e in es:
        denom = denom + jnp.sum(e, axis=-1, keepdims=True)
    inv = 1.0 / denom
    o = None
    for e, vp in zip(es, v_parts):
        part = jnp.dot((e * inv).astype(BF16), vp, preferred_element_type=F32)
        o = part if o is None else o + part
    for g, h in enumerate(heads):
        o_ref[:, h * dh:(h + 1) * dh] = o[g * rows:(g + 1) * rows].astype(o_ref.dtype)


def _gqa_lat_kernel(sink_ref, q_ref, kp_ref, ko_ref, kn_ref, vp_ref, vo_ref, vn_ref,
                    kc_ref, vc_ref, o_ref, *, layer_j, n_kv, n_blk):
    n = pl.program_id(1)
    dh = HEAD_DIM
    shape = (KV_GROUP * BLOCK, 3 * BLOCK)
    qi = lax.broadcasted_iota(I32, shape, 0) % BLOCK
    c = lax.broadcasted_iota(I32, shape, 1)
    rel = (c - BLOCK) - qi
    kpos = n * BLOCK - BLOCK + c
    valid = (jnp.abs(rel) <= WINDOW) & (kpos >= 0) & (kpos < n_blk * BLOCK)
    for kvh in range(n_kv):
        sl = slice(kvh * dh, (kvh + 1) * dh)
        k_lat = jnp.concatenate([kp_ref[:, sl], ko_ref[:, sl], kn_ref[:, sl]], axis=0)
        v_lat = jnp.concatenate([vp_ref[:, sl], vo_ref[:, sl], vn_ref[:, sl]], axis=0)
        _gqa_head_group(q_ref, [k_lat, kc_ref[:, sl]], [v_lat, vc_ref[:, sl]],
                        valid, sink_ref, layer_j, kvh, o_ref)


def _gqa_ctx_kernel(sink_ref, q_ref, kc_ref, vc_ref, o_ref, *, layer_j, n_kv):
    dh = HEAD_DIM
    for kvh in range(n_kv):
        sl = slice(kvh * dh, (kvh + 1) * dh)
        _gqa_head_group(q_ref, [kc_ref[:, sl]], [vc_ref[:, sl]], None, sink_ref, layer_j, kvh, o_ref)


def _gqa_attention(cfg, qkv, sink, layer_j, need_ctx):
    t = qkv.shape[0]
    d = cfg.d
    dh = HEAD_DIM
    n_kv = d // dh // KV_GROUP
    nk = n_kv * dh
    n_blk = cfg.seq // BLOCK
    kcol, vcol = d // nk, d // nk + 1
    ctx_blk0 = cfg.t_lat // cfg.ctx_len
    smem = pl.BlockSpec(memory_space=pltpu.SMEM)

    def band(col, shift):
        return pl.BlockSpec(
            (BLOCK, nk),
            lambda b, n: (b * n_blk + jnp.clip(n + shift, 0, n_blk - 1), col))

    def ctx_spec(col):
        return pl.BlockSpec((cfg.ctx_len, nk), lambda b, n: (ctx_blk0 + b, col))

    o_lat = pl.pallas_call(
        functools.partial(_gqa_lat_kernel, layer_j=layer_j, n_kv=n_kv, n_blk=n_blk),
        grid=(cfg.batch, n_blk),
        in_specs=[smem, pl.BlockSpec((BLOCK, d), lambda b, n: (b * n_blk + n, 0)),
                  band(kcol, -1), band(kcol, 0), band(kcol, 1),
                  band(vcol, -1), band(vcol, 0), band(vcol, 1),
                  ctx_spec(kcol), ctx_spec(vcol)],
        out_specs=pl.BlockSpec((BLOCK, d), lambda b, n: (b * n_blk + n, 0)),
        out_shape=jax.ShapeDtypeStruct((cfg.t_lat, d), BF16),
        compiler_params=_params("parallel", "parallel"),
        name="gqa_latent",
    )(sink, qkv, qkv, qkv, qkv, qkv, qkv, qkv, qkv, qkv)
    if not need_ctx:
        return o_lat, None

    def ctx1(col):
        return pl.BlockSpec((cfg.ctx_len, nk), lambda b: (ctx_blk0 + b, col))

    o_ctx = pl.pallas_call(
        functools.partial(_gqa_ctx_kernel, layer_j=layer_j, n_kv=n_kv),
        grid=(cfg.batch,),
        in_specs=[smem, pl.BlockSpec((cfg.ctx_len, d), lambda b: (ctx_blk0 + b, 0)),
                  ctx1(kcol), ctx1(vcol)],
        out_specs=pl.BlockSpec((cfg.ctx_len, d), lambda b: (b, 0)),
        out_shape=jax.ShapeDtypeStruct((cfg.t_ctx, d), BF16),
        compiler_params=_params("parallel"),
        name="gqa_context",
    )(sink, qkv, qkv, qkv)
    return o_lat, o_ctx


def _diff_kernel(lam_ref, g_ref, q_ref, *refs, lam_init, n_parts):
    k_refs, v_refs, o_ref = refs[:n_parts], refs[n_parts:2 * n_parts], refs[2 * n_parts]
    dh = HEAD_DIM
    scale = dh ** -0.5
    lf = lam_ref[...]
    lam = (jnp.exp(jnp.sum(lf[0:1] * lf[1:2], keepdims=True))
           - jnp.exp(jnp.sum(lf[2:3] * lf[3:4], keepdims=True)) + lam_init)
    probs = []
    for c in range(2):
        qc = q_ref[:, c * dh:(c + 1) * dh]
        ss = [lax.dot_general(qc, k_ref[:, c * dh:(c + 1) * dh], _NT,
                              preferred_element_type=F32) * scale for k_ref in k_refs]
        m = None
        for s in ss:
            mx = jnp.max(s, axis=-1, keepdims=True)
            m = mx if m is None else jnp.maximum(m, mx)
        es = [jnp.exp(s - m) for s in ss]
        denom = None
        for e in es:
            sm = jnp.sum(e, axis=-1, keepdims=True)
            denom = sm if denom is None else denom + sm
        probs.append((es, 1.0 / denom))
    (e0, inv0), (e1, inv1) = probs
    inv1 = inv1 * lam
    o = None
    for p in range(n_parts):
        a = (e0[p] * inv0 - e1[p] * inv1).astype(BF16)
        part = jnp.dot(a, v_refs[p][...], preferred_element_type=F32)
        o = part if o is None else o + part
    y = o * lax.rsqrt(jnp.mean(o * o, axis=-1, keepdims=True) + NORM_EPS)
    o_ref[...] = ((y * g_ref[...]) * (1.0 - lam_init)).astype(o_ref.dtype)


def _diff_attention(cfg, qkv, lam_vecs, subln_g, layer_j, lam_init, need_ctx):
    d = cfg.d
    dh = HEAD_DIM
    hw = 2 * dh
    nh = d // hw
    tq = _tile(cfg.seq, 512)
    qpb = cfg.seq // tq
    ctx_blk0 = cfg.t_lat // cfg.ctx_len
    lam_spec3 = pl.BlockSpec((None, 4, dh), lambda b, h, i: (layer_j, 0, 0))
    g_spec3 = pl.BlockSpec((None, 1, hw), lambda b, h, i: (layer_j, 0, 0))
    o_lat = pl.pallas_call(
        functools.partial(_diff_kernel, lam_init=lam_init, n_parts=2),
        grid=(cfg.batch, nh, qpb),
        in_specs=[lam_spec3, g_spec3,
                  pl.BlockSpec((tq, hw), lambda b, h, i: (b * qpb + i, h)),
                  pl.BlockSpec((cfg.seq, hw), lambda b, h, i: (b, nh + h)),
                  pl.BlockSpec((cfg.ctx_len, hw), lambda b, h, i: (ctx_blk0 + b, nh + h)),
                  pl.BlockSpec((cfg.seq, hw), lambda b, h, i: (b, 2 * nh + h)),
                  pl.BlockSpec((cfg.ctx_len, hw), lambda b, h, i: (ctx_blk0 + b, 2 * nh + h))],
        out_specs=pl.BlockSpec((tq, hw), lambda b, h, i: (b * qpb + i, h)),
        out_shape=jax.ShapeDtypeStruct((cfg.t_lat, d), BF16),
        compiler_params=_params("parallel", "parallel", "arbitrary"),
        name="diff_latent",
    )(lam_vecs, subln_g, qkv, qkv, qkv, qkv, qkv)
    if not need_ctx:
        return o_lat, None
    o_ctx = pl.pallas_call(
        functools.partial(_diff_kernel, lam_init=lam_init, n_parts=1),
        grid=(cfg.batch, nh),
        in_specs=[pl.BlockSpec((None, 4, dh), lambda b, h: (layer_j, 0, 0)),
                  pl.BlockSpec((None, 1, hw), lambda b, h: (layer_j, 0, 0)),
                  pl.BlockSpec((cfg.ctx_len, hw), lambda b, h: (ctx_blk0 + b, h)),
                  pl.BlockSpec((cfg.ctx_len, hw), lambda b, h: (ctx_blk0 + b, nh + h)),
                  pl.BlockSpec((cfg.ctx_len, hw), lambda b, h: (ctx_blk0 + b, 2 * nh + h))],
        out_specs=pl.BlockSpec((cfg.ctx_len, hw), lambda b, h: (b, h)),
        out_shape=jax.ShapeDtypeStruct((cfg.t_ctx, d), BF16),
        compiler_params=_params("parallel", "parallel"),
        name="diff_context",
    )(lam_vecs, subln_g, qkv, qkv, qkv)
    return o_lat, o_ctx


def _conv_kernel(gb_ref, gc_ref, u_ref, gcp_ref, up_ref, gcn_ref, un_ref, w_ref, o_ref,
                 *, blocks_per_seq, lat_blocks):
    i = pl.program_id(1)
    rb = gc_ref.shape[0]
    v = gc_ref[...].astype(F32) * u_ref[...].astype(F32)
    pos = jnp.where(i < lat_blocks, i % blocks_per_seq, 0)
    last = jnp.where(i < lat_blocks, blocks_per_seq - 1, 0)
    prev_row = jnp.where(pos > 0, gcp_ref[7:8, :].astype(F32) * up_ref[7:8, :].astype(F32), 0.0)
    next_row = jnp.where(pos < last, gcn_ref[0:1, :].astype(F32) * un_ref[0:1, :].astype(F32), 0.0)
    row = lax.broadcasted_iota(I32, v.shape, 0)
    v_prev = jnp.where(row == 0, prev_row, pltpu.roll(v, 1, 0))
    v_next = jnp.where(row == rb - 1, next_row, pltpu.roll(v, rb - 1, 0))
    w = w_ref[...]
    z = w[0:1] * v_prev + w[1:2] * v + w[2:3] * v_next
    o_ref[...] = (gb_ref[...].astype(F32) * z).astype(o_ref.dtype)


def _conv_gate(cfg, proj, conv_w, layer_j):
    t = proj.shape[0]
    d = cfg.d
    rb = _tile(cfg.ctx_len, 256)
    cb = _tile(d, 1024)
    ncb = d // cb
    sub = rb // 8
    last8 = t // 8 - 1
    blocks_per_seq = cfg.seq // rb
    lat_blocks = cfg.t_lat // rb
    assert cfg.ctx_len == rb, "context sequences must be exactly one row block"

    def main(part):
        return pl.BlockSpec((rb, cb), lambda j, i: (i, part * ncb + j))

    def halo(part, nxt):
        if nxt:
            return pl.BlockSpec((8, cb), lambda j, i: (jnp.minimum((i + 1) * sub, last8), part * ncb + j))
        return pl.BlockSpec((8, cb), lambda j, i: (jnp.maximum(i * sub - 1, 0), part * ncb + j))

    return pl.pallas_call(
        functools.partial(_conv_kernel, blocks_per_seq=blocks_per_seq, lat_blocks=lat_blocks),
        grid=(ncb, t // rb),
        in_specs=[main(0), main(1), main(2), halo(1, False), halo(2, False),
                  halo(1, True), halo(2, True),
                  pl.BlockSpec((None, CONV_W, cb), lambda j, i: (layer_j, 0, j))],
        out_specs=pl.BlockSpec((rb, cb), lambda j, i: (i, j)),
        out_shape=jax.ShapeDtypeStruct((t, d), BF16),
        compiler_params=_params("parallel", "parallel"),
        name="conv_gate",
    )(proj, proj, proj, proj, proj, proj, proj, conv_w)


def _route_kernel(logit_ref, bias_ref, eidx_ref, wts_ref, mask_ref):
    scores = jax.nn.sigmoid(logit_ref[...])
    biased = scores + bias_ref[...]
    rows, ne = scores.shape
    per_group = ne // N_GROUPS
    lane_i = lax.broadcasted_iota(I32, (rows, ne), 1)
    grp = lane_i // per_group
    lane = lane_i.astype(F32)
    neg = jnp.float32(-jnp.inf)
    gscores = []
    for g in range(N_GROUPS):
        in_g = grp == g
        vals = jnp.where(in_g, biased, neg)
        m1 = jnp.max(vals, axis=-1, keepdims=True)
        i1 = jnp.min(jnp.where(vals == m1, lane, float(ne)), axis=-1, keepdims=True)
        m2 = jnp.max(jnp.where(lane == i1, neg, vals), axis=-1, keepdims=True)
        gscores.append(m1 + m2)
    emask = jnp.zeros((rows, ne), jnp.bool_)
    for g in range(N_GROUPS):
        beaten = jnp.zeros((rows, 1), I32)
        for h in range(N_GROUPS):
            if h == g:
                continue
            wins = (gscores[h] > gscores[g]) | ((gscores[h] == gscores[g]) & (h < g))
            beaten = beaten + wins.astype(I32)
        emask = emask | ((grp == g) & (beaten < TOPK_GROUPS))
    cand = jnp.where(emask, biased, MASK_VALUE)
    k_lane = lax.broadcasted_iota(I32, (rows, TOP_K), 1)
    eidx = jnp.zeros((rows, TOP_K), I32)
    wsel = jnp.zeros((rows, TOP_K), F32)
    chosen = jnp.zeros((rows, ne), jnp.bool_)
    for k in range(TOP_K):
        m = jnp.max(cand, axis=-1, keepdims=True)
        idx = jnp.min(jnp.where(cand == m, lane, float(ne)), axis=-1, keepdims=True)
        hit = lane == idx
        sc = jnp.sum(jnp.where(hit, scores, 0.0), axis=-1, keepdims=True)
        eidx = jnp.where(k_lane == k, idx.astype(I32), eidx)
        wsel = jnp.where(k_lane == k, sc, wsel)
        chosen = chosen | hit
        cand = jnp.where(hit, neg, cand)
    total = jnp.sum(wsel, axis=-1, keepdims=True)
    eidx_ref[...] = eidx
    wts_ref[...] = wsel / total * ROUTED_SCALE
    mask_ref[...] = chosen.astype(I32)


def _route(logits, router_b, layer):
    t, ne = logits.shape
    rb = _tile(t, 256)
    return pl.pallas_call(
        _route_kernel,
        grid=(t // rb,),
        in_specs=[pl.BlockSpec((rb, ne), lambda i: (i, 0)),
                  pl.BlockSpec((None, 1, ne), lambda i: (layer, 0, 0))],
        out_specs=[pl.BlockSpec((rb, TOP_K), lambda i: (i, 0)),
                   pl.BlockSpec((rb, TOP_K), lambda i: (i, 0)),
                   pl.BlockSpec((rb, ne), lambda i: (i, 0))],
        out_shape=[jax.ShapeDtypeStruct((t, TOP_K), I32),
                   jax.ShapeDtypeStruct((t, TOP_K), F32),
                   jax.ShapeDtypeStruct((t, ne), I32)],
        compiler_params=_params("parallel"),
        name="route",
    )(logits, router_b)


def _dispatch_plan(eidx, mask, tile):
    t, ne = mask.shape
    n_rows = t * TOP_K
    n_tiles = n_rows // tile
    n_work = n_tiles + ne
    counts = jnp.sum(mask, axis=0)
    ends = jnp.cumsum(counts)
    offs = ends - counts
    rank = jnp.cumsum(mask, axis=0) - mask
    pos = jnp.take_along_axis(offs[None, :] + rank, eidx, axis=1).astype(I32)
    first_tile = offs // tile
    last_tile = jnp.maximum(ends - 1, offs) // tile
    n_items = jnp.where(counts > 0, last_tile - first_tile + 1, 0)
    item_end = jnp.cumsum(n_items)
    item_off = item_end - n_items
    total = item_end[-1]
    w = jnp.arange(n_work, dtype=I32)
    e_of = jnp.minimum(jnp.searchsorted(item_end, w, side="right"), ne - 1).astype(I32)
    last_e = jnp.max(jnp.where(counts > 0, jnp.arange(ne), 0)).astype(I32)
    live = w < total
    e_of = jnp.where(live, e_of, last_e)
    tile_of = jnp.where(live, first_tile[e_of] + (w - item_off[e_of]), n_tiles - 1).astype(I32)
    lo = jnp.clip(offs[e_of] - tile_of * tile, 0, tile)
    hi = jnp.clip(ends[e_of] - tile_of * tile, 0, tile)
    lo = jnp.where(live, lo, 0).astype(I32)
    hi = jnp.where(live, hi, 0).astype(I32)
    prev_tile = jnp.concatenate([jnp.full((1,), -1, I32), tile_of[:-1]])
    prev_e = jnp.concatenate([jnp.full((1,), -1, I32), e_of[:-1]])
    new_tile = (tile_of != prev_tile).astype(I32)
    new_e = (e_of != prev_e).astype(I32)
    return pos, (tile_of, e_of, lo, hi, new_tile, new_e)


def _dispatch_kernel(pos_ref, f_ref, xs_ref, sem):
    rows = f_ref.shape[0]

    def issue(r, carry):
        for k in range(TOP_K):
            p = pos_ref[0, r * TOP_K + k]
            pltpu.make_async_copy(f_ref.at[pl.ds(r, 1)], xs_ref.at[pl.ds(p, 1)], sem).start()
        return carry

    lax.fori_loop(0, rows, issue, 0)
    for k in range(TOP_K):
        pltpu.make_async_copy(f_ref, xs_ref.at[pl.ds(0, rows)], sem).wait()


def _dispatch(fp, pos):
    t, dw = fp.shape
    rb = _tile(t, 256)
    nb = t // rb
    return pl.pallas_call(
        _dispatch_kernel,
        grid=(nb,),
        in_specs=[pl.BlockSpec((None, 1, rb * TOP_K), lambda i: (i, 0, 0), memory_space=pltpu.SMEM),
                  pl.BlockSpec((rb, dw), lambda i: (i, 0))],
        out_specs=pl.BlockSpec(memory_space=pl.ANY),
        out_shape=jax.ShapeDtypeStruct((t * TOP_K, dw), U32),
        scratch_shapes=[pltpu.SemaphoreType.DMA(())],
        compiler_params=_params("arbitrary"),
        name="moe_dispatch",
    )(pos.reshape(nb, 1, rb * TOP_K), fp)


def _row_mask(lo_ref, hi_ref, w, rows):
    r = lax.broadcasted_iota(I32, (rows, 1), 0)
    return (r >= lo_ref[w]) & (r < hi_ref[w])


def _gmm_up_kernel(tile_ref, e_ref, lo_ref, hi_ref, nt_ref, ne_ref,
                   x_ref, wg_ref, wu_ref, o_ref, wgb_ref, wub_ref):
    w = pl.program_id(0)

    @pl.when(ne_ref[w] == 1)
    def _():
        wgb_ref[...] = wg_ref[...].astype(BF16)
        wub_ref[...] = wu_ref[...].astype(BF16)

    lo, hi = _unpack_bf16_pair(x_ref[...])
    lo, hi = lo.astype(BF16), hi.astype(BF16)
    half = lo.shape[1]
    hg = (jnp.dot(lo, wgb_ref[:half], preferred_element_type=F32)
          + jnp.dot(hi, wgb_ref[half:], preferred_element_type=F32))
    hu = (jnp.dot(lo, wub_ref[:half], preferred_element_type=F32)
          + jnp.dot(hi, wub_ref[half:], preferred_element_type=F32))
    act = (_silu(hg) * hu).astype(o_ref.dtype)
    m = _row_mask(lo_ref, hi_ref, w, act.shape[0])

    @pl.when(nt_ref[w] == 1)
    def _():
        o_ref[...] = jnp.where(m, act, jnp.zeros_like(act))

    @pl.when(nt_ref[w] == 0)
    def _():
        o_ref[...] = jnp.where(m, act, o_ref[...])


def _gmm_down_kernel(tile_ref, e_ref, lo_ref, hi_ref, nt_ref, ne_ref,
                     h_ref, wd_ref, o_ref, wdb_ref):
    w = pl.program_id(0)

    @pl.when(ne_ref[w] == 1)
    def _():
        wdb_ref[...] = wd_ref[...].astype(BF16)

    y = jnp.dot(h_ref[...], wdb_ref[...], preferred_element_type=F32)
    half = y.shape[1] // 2
    yp = _pack_bf16_pair(y[:, :half], y[:, half:])
    m = _row_mask(lo_ref, hi_ref, w, yp.shape[0])

    @pl.when(nt_ref[w] == 1)
    def _():
        o_ref[...] = jnp.where(m, yp, jnp.zeros_like(yp))

    @pl.when(nt_ref[w] == 0)
    def _():
        o_ref[...] = jnp.where(m, yp, o_ref[...])


def _gmm_up(xs_sorted, plan, exp_gate, exp_up, layer):
    n_rows, dw = xs_sorted.shape
    d, ff = exp_gate.shape[2], exp_gate.shape[3]
    n_work = plan[0].shape[0]
    w_spec = pl.BlockSpec((None, None, d, ff), lambda w, tl, e, *_: (layer, e[w], 0, 0))
    return pl.pallas_call(
        _gmm_up_kernel,
        grid_spec=pltpu.PrefetchScalarGridSpec(
            num_scalar_prefetch=6,
            grid=(n_work,),
            in_specs=[pl.BlockSpec((GMM_TILE, dw), lambda w, tl, *_: (tl[w], 0)), w_spec, w_spec],
            out_specs=pl.BlockSpec((GMM_TILE, ff), lambda w, tl, *_: (tl[w], 0)),
            scratch_shapes=[pltpu.VMEM((d, ff), BF16), pltpu.VMEM((d, ff), BF16)],
        ),
        out_shape=jax.ShapeDtypeStruct((n_rows, ff), BF16),
        compiler_params=_params("arbitrary"),
        name="moe_up",
    )(*plan, xs_sorted, exp_gate, exp_up)


def _gmm_down(h_sorted, plan, exp_down, layer):
    n_rows, ff = h_sorted.shape
    d = exp_down.shape[3]
    n_work = plan[0].shape[0]
    return pl.pallas_call(
        _gmm_down_kernel,
        grid_spec=pltpu.PrefetchScalarGridSpec(
            num_scalar_prefetch=6,
            grid=(n_work,),
            in_specs=[pl.BlockSpec((GMM_TILE, ff), lambda w, tl, *_: (tl[w], 0)),
                      pl.BlockSpec((None, None, ff, d), lambda w, tl, e, *_: (layer, e[w], 0, 0))],
            out_specs=pl.BlockSpec((GMM_TILE, d // 2), lambda w, tl, *_: (tl[w], 0)),
            scratch_shapes=[pltpu.VMEM((ff, d), BF16)],
        ),
        out_shape=jax.ShapeDtypeStruct((n_rows, d // 2), U32),
        compiler_params=_params("arbitrary"),
        name="moe_down",
    )(*plan, h_sorted, exp_down)


def _combine_kernel(pos_ref, wts_ref, ys_ref, x_ref, hsh_ref, sd_ref, g2_ref,
                    ng_ref, sc_ref, sh_ref, xo_ref, a_ref, buf_ref, sem, *, final):
    rows = x_ref.shape[0]

    def issue(r, carry):
        for k in range(TOP_K):
            p = pos_ref[0, r * TOP_K + k]
            pltpu.make_async_copy(ys_ref.at[pl.ds(p, 1)], buf_ref.at[k, pl.ds(r, 1)], sem).start()
        return carry

    lax.fori_loop(0, rows, issue, 0)
    acc = jnp.dot(hsh_ref[...], sd_ref[...], preferred_element_type=F32)
    for k in range(TOP_K):
        pltpu.make_async_copy(ys_ref.at[pl.ds(0, rows)], buf_ref.at[k], sem).wait()
    wts = wts_ref[...]
    for k in range(TOP_K):
        lo, hi = _unpack_bf16_pair(buf_ref[k])
        acc = acc + wts[:, k:k + 1] * jnp.concatenate([lo, hi], axis=1)
    xn = x_ref[...] + g2_ref[...] * acc
    xo_ref[...] = xn
    if final:
        y = xn * lax.rsqrt(jnp.mean(xn * xn, axis=-1, keepdims=True) + NORM_EPS)
        a_ref[...] = (y * ng_ref[...]).astype(a_ref.dtype)
    else:
        a_ref[...] = _norm_mod(xn, ng_ref[...], sc_ref[...], sh_ref[...]).astype(a_ref.dtype)


def _combine(cfg, xs, ys_sorted, pos, wts, hsh, sh_down_bf, mod, layer, next_g, final):
    t, d = xs.shape
    ff = hsh.shape[1]
    rb = _tile(cfg.ctx_len, 128)
    nb = t // rb
    nxt = layer + 1
    if final:
        g_spec = pl.BlockSpec((1, d), lambda i: (0, 0))
        sc_spec = _mod_spec(cfg, layer, 1, rb, d)
        sh_spec = _mod_spec(cfg, layer, 0, rb, d)
    else:
        g_spec = pl.BlockSpec((None, 1, d), lambda i: (nxt, 0, 0))
        sc_spec = _mod_spec(cfg, nxt, 1, rb, d)
        sh_spec = _mod_spec(cfg, nxt, 0, rb, d)
    return pl.pallas_call(
        functools.partial(_combine_kernel, final=final),
        grid=(nb,),
        in_specs=[pl.BlockSpec((None, 1, rb * TOP_K), lambda i: (i, 0, 0), memory_space=pltpu.SMEM),
                  pl.BlockSpec((rb, TOP_K), lambda i: (i, 0)),
                  pl.BlockSpec(memory_space=pl.ANY),
                  pl.BlockSpec((rb, d), lambda i: (i, 0)),
                  pl.BlockSpec((rb, ff), lambda i: (i, 0)),
                  pl.BlockSpec((None, ff, d), lambda i: (layer, 0, 0)),
                  _mod_spec(cfg, layer, 5, rb, d),
                  g_spec, sc_spec, sh_spec],
        out_specs=[pl.BlockSpec((rb, d), lambda i: (i, 0)),
                   pl.BlockSpec((rb, d), lambda i: (i, 0))],
        out_shape=[jax.ShapeDtypeStruct((t, d), F32),
                   jax.ShapeDtypeStruct((t, d), F32 if final else BF16)],
        scratch_shapes=[pltpu.VMEM((TOP_K, rb, d // 2), U32), pltpu.SemaphoreType.DMA(())],
        compiler_params=_params("arbitrary"),
        name="moe_combine",
    )(pos.reshape(nb, 1, rb * TOP_K), wts, ys_sorted, xs, hsh, sh_down_bf, mod,
      next_g, mod, mod)


def _rope_tables(seq):
    rows = seq // GRID_W
    row = jnp.broadcast_to(jnp.arange(rows)[:, None], (rows, GRID_W)).reshape(-1).astype(F32)
    col = jnp.broadcast_to(jnp.arange(GRID_W)[None, :], (rows, GRID_W)).reshape(-1).astype(F32)
    n_freq = HEAD_DIM // 4
    inv = ROPE_THETA ** (-jnp.arange(n_freq, dtype=F32) / n_freq)
    ang = jnp.concatenate([row[:, None] * inv, col[:, None] * inv], axis=-1)
    cos, sin = jnp.cos(ang), jnp.sin(ang)
    return jnp.concatenate([cos, cos], axis=-1), jnp.concatenate([-sin, sin], axis=-1)


def kernel(x, c, ctx, c_ctx, ada_down, ada_up, ada_b, norm1_g, norm2_g, a_w_qkv, a_w_o, a_sink,
           b_w_qkv, b_w_o, b_lam, b_subln_g, c_w_in, c_conv, c_w_out, router_w, router_b,
           exp_gate, exp_up, exp_down, sh_gate, sh_up, sh_down, final_g):
    batch, seq, d = x.shape
    ctx_len = ctx.shape[1]
    depth = ada_down.shape[0]
    cfg = _Cfg(batch, seq, ctx_len, d)
    assert batch + 1 <= MOD_ROWS

    xs = jnp.concatenate([x.reshape(-1, d), ctx.reshape(-1, d)], axis=0)
    src = jnp.concatenate([c, c_ctx[None, :], jnp.zeros((MOD_ROWS - batch - 1, d), F32)], axis=0)
    mod = _ada_table(src, ada_down, ada_up, ada_b).reshape(depth * MOD_ROWS, 1, 6 * d)
    cos_t, sin_t = _rope_tables(seq)
    norm1_g3 = norm1_g.reshape(depth, 1, d)
    norm2_g3 = norm2_g.reshape(depth, 1, d)
    router_b3 = router_b.reshape(depth, 1, -1)
    b_subln_g3 = b_subln_g.reshape(b_subln_g.shape[0], 1, -1)
    sh_gate_bf, sh_up_bf, sh_down_bf = (w.astype(BF16) for w in (sh_gate, sh_up, sh_down))

    a = _norm1(cfg, xs, norm1_g3, mod, 0)
    out = None
    for i in range(depth):
        kind, j = i % N_MIXERS, i // N_MIXERS
        need_ctx = i < depth - 1
        if kind == 0:
            nq = d
            nk = d // KV_GROUP
            qkv = _proj(cfg, a, a_w_qkv, j, rope=(cos_t, sin_t, nq + nk))
            o_lat, o_ctx = _gqa_attention(cfg, qkv, a_sink, j, need_ctx)
            w_o = a_w_o
        elif kind == 1:
            lam_init = 0.8 - 0.6 * math.exp(-0.3 * i)
            qkv = _proj(cfg, a, b_w_qkv, j, rope=(cos_t, sin_t, 2 * d))
            o_lat, o_ctx = _diff_attention(cfg, qkv, b_lam, b_subln_g3, j, lam_init, need_ctx)
            w_o = b_w_o
        else:
            pr = _proj(cfg, a, c_w_in, j)
            o = _conv_gate(cfg, pr, c_conv, j)
            w_o = c_w_out
        if kind != 2:
            if o_ctx is None:
                o_ctx = jnp.zeros((cfg.t_ctx, d), BF16)
            o = jnp.concatenate([o_lat, o_ctx], axis=0)
        xs = _proj_resid(cfg, o, w_o, j, xs, mod, i, 2)

        fp, logits, hsh = _norm2(cfg, xs, norm2_g3, mod, i, router_w, sh_gate_bf, sh_up_bf)
        eidx, wts, mask = _route(logits, router_b3, i)
        pos, plan = _dispatch_plan(eidx, mask, GMM_TILE)
        xs_sorted = _dispatch(fp, pos)
        h_sorted = _gmm_up(xs_sorted, plan, exp_gate, exp_up, i)
        ys_sorted = _gmm_down(h_sorted, plan, exp_down, i)
        final = i == depth - 1
        next_g = final_g.reshape(1, d) if final else norm1_g3
        xs, a = _combine(cfg, xs, ys_sorted, pos, wts, hsh, sh_down_bf, mod, i, next_g, final)
        out = a
    return out[:cfg.t_lat].reshape(batch, seq, d)
```

```python
import functools
import math

import jax
import jax.numpy as jnp
from jax import lax
from jax.experimental import pallas as pl
from jax.experimental.pallas import tpu as pltpu

F32 = jnp.float32
BF16 = jnp.bfloat16
U32 = jnp.uint32
I32 = jnp.int32

HEAD_DIM = 128
GRID_W = 64
ROPE_THETA = 10000.0
NORM_EPS = 1e-6
MASK_VALUE = -1e30
WINDOW = 128
BLOCK = 128
CONV_W = 3
N_MIXERS = 3
N_EXPERTS = 64
TOP_K = 8
N_GROUPS = 8
TOPK_GROUPS = 4
ROUTED_SCALE = 2.5
KV_GROUP = 4
MOD_ROWS = 8

LANE = 128
VMEM_LIMIT_BYTES = 56 * 1024 * 1024
GMM_TILE = 256
HI_MASK = 0xFFFF0000


def _params(*sem):
    return pltpu.CompilerParams(dimension_semantics=sem, vmem_limit_bytes=VMEM_LIMIT_BYTES)


def _tile(n, pref):
    t = min(n, pref)
    while n % t:
        t //= 2
    return t


def _silu(v):
    return v * jax.nn.sigmoid(v)


def _pack_bf16_pair(lo, hi):
    lo_b = lax.bitcast_convert_type(lo.astype(BF16).astype(F32), U32)
    hi_b = lax.bitcast_convert_type(hi.astype(BF16).astype(F32), U32)
    return (lo_b >> 16) | (hi_b & jnp.uint32(HI_MASK))


def _unpack_bf16_pair(u):
    lo = lax.bitcast_convert_type(u << 16, F32)
    hi = lax.bitcast_convert_type(u & jnp.uint32(HI_MASK), F32)
    return lo, hi


def _ada_kernel(src_ref, down_ref, up_ref, b_ref, out_ref, t1_ref):
    @pl.when(pl.program_id(1) == 0)
    def _():
        s = _silu(src_ref[...]).astype(BF16)
        t1 = jnp.dot(s, down_ref[...].astype(BF16), preferred_element_type=F32)
        t1_ref[...] = t1.astype(BF16)

    out_ref[...] = jnp.dot(t1_ref[...], up_ref[...].astype(BF16),
                           preferred_element_type=F32) + b_ref[...]


def _ada_table(src, ada_down, ada_up, ada_b):
    depth, d, rank = ada_down.shape
    n6 = ada_up.shape[2]
    tn = _tile(n6, 2048)
    return pl.pallas_call(
        _ada_kernel,
        grid=(depth, n6 // tn),
        in_specs=[
            pl.BlockSpec((MOD_ROWS, d), lambda l, j: (0, 0)),
            pl.BlockSpec((None, d, rank), lambda l, j: (l, 0, 0)),
            pl.BlockSpec((None, rank, tn), lambda l, j: (l, 0, j)),
            pl.BlockSpec((None, 1, tn), lambda l, j: (l, 0, j)),
        ],
        out_specs=pl.BlockSpec((None, MOD_ROWS, tn), lambda l, j: (l, 0, j)),
        out_shape=jax.ShapeDtypeStruct((depth, MOD_ROWS, n6), F32),
        scratch_shapes=[pltpu.VMEM((MOD_ROWS, rank), BF16)],
        compiler_params=_params("arbitrary", "arbitrary"),
        name="ada_table",
    )(src, ada_down, ada_up, ada_b.reshape(depth, 1, n6))


def _norm_mod(xf, g, sc, sh):
    y = xf * lax.rsqrt(jnp.mean(xf * xf, axis=-1, keepdims=True) + NORM_EPS)
    return (y * g) * (1.0 + sc) + sh


class _Cfg:
    def __init__(self, batch, seq, ctx_len, d):
        self.batch, self.seq, self.ctx_len, self.d = batch, seq, ctx_len, d
        self.t_lat = batch * seq
        self.t_ctx = batch * ctx_len
        self.t = self.t_lat + self.t_ctx

    def group_of_rows(self, rows_per_block):
        per_seq = self.seq // rows_per_block
        assert per_seq * rows_per_block == self.seq
        assert self.t_lat % rows_per_block == 0 and self.t_ctx % rows_per_block == 0
        batch = self.batch
        return lambda i: jnp.minimum(i // per_seq, batch)


def _mod_spec(cfg, layer, chunk, rows_per_block, width, col_of=None):
    grp = cfg.group_of_rows(rows_per_block)
    per_chunk = cfg.d // width
    if col_of is None:
        return pl.BlockSpec((None, 1, width),
                            lambda i: (layer * MOD_ROWS + grp(i), 0, chunk * per_chunk))
    return pl.BlockSpec((None, 1, width),
                        lambda j, i: (layer * MOD_ROWS + grp(i), 0, chunk * per_chunk + j))


def _norm1_kernel(x_ref, g_ref, sc_ref, sh_ref, a_ref):
    a_ref[...] = _norm_mod(x_ref[...], g_ref[...], sc_ref[...], sh_ref[...]).astype(BF16)


def _norm1(cfg, xs, norm_g, mod, layer):
    t, d = xs.shape
    rb = _tile(cfg.ctx_len, 256)
    return pl.pallas_call(
        _norm1_kernel,
        grid=(t // rb,),
        in_specs=[
            pl.BlockSpec((rb, d), lambda i: (i, 0)),
            pl.BlockSpec((None, 1, d), lambda i: (layer, 0, 0)),
            _mod_spec(cfg, layer, 1, rb, d),
            _mod_spec(cfg, layer, 0, rb, d),
        ],
        out_specs=pl.BlockSpec((rb, d), lambda i: (i, 0)),
        out_shape=jax.ShapeDtypeStruct((t, d), BF16),
        compiler_params=_params("parallel"),
        name="norm1",
    )(xs, norm_g, mod, mod)


def _norm2_kernel(x_ref, g_ref, sc_ref, sh_ref, rw_ref, sg_ref, su_ref,
                  fp_ref, logit_ref, hsh_ref):
    f = _norm_mod(x_ref[...], g_ref[...], sc_ref[...], sh_ref[...])
    half = f.shape[1] // 2
    fp_ref[...] = _pack_bf16_pair(f[:, :half], f[:, half:])
    fb = f.astype(BF16)
    f_lo = (f - fb.astype(F32)).astype(BF16)
    ne = logit_ref.shape[1]
    both = jnp.dot(fb, rw_ref[...], preferred_element_type=F32)
    logit_ref[...] = (both[:, :ne] + both[:, ne:]
                      + jnp.dot(f_lo, rw_ref[:, :ne], preferred_element_type=F32))
    hg = jnp.dot(fb, sg_ref[...], preferred_element_type=F32)
    hu = jnp.dot(fb, su_ref[...], preferred_element_type=F32)
    hsh_ref[...] = (_silu(hg) * hu).astype(BF16)


def _norm2(cfg, xs, norm_g, mod, layer, router_w, sh_gate_bf, sh_up_bf):
    t, d = xs.shape
    ne = router_w.shape[2] // 2
    ff = sh_gate_bf.shape[2]
    rb = _tile(cfg.ctx_len, 256)
    return pl.pallas_call(
        _norm2_kernel,
        grid=(t // rb,),
        in_specs=[
            pl.BlockSpec((rb, d), lambda i: (i, 0)),
            pl.BlockSpec((None, 1, d), lambda i: (layer, 0, 0)),
            _mod_spec(cfg, layer, 4, rb, d),
            _mod_spec(cfg, layer, 3, rb, d),
            pl.BlockSpec((None, d, 2 * ne), lambda i: (layer, 0, 0)),
            pl.BlockSpec((None, d, ff), lambda i: (layer, 0, 0)),
            pl.BlockSpec((None, d, ff), lambda i: (layer, 0, 0)),
        ],
        out_specs=[
            pl.BlockSpec((rb, d // 2), lambda i: (i, 0)),
            pl.BlockSpec((rb, ne), lambda i: (i, 0)),
            pl.BlockSpec((rb, ff), lambda i: (i, 0)),
        ],
        out_shape=[
            jax.ShapeDtypeStruct((t, d // 2), U32),
            jax.ShapeDtypeStruct((t, ne), F32),
            jax.ShapeDtypeStruct((t, ff), BF16),
        ],
        compiler_params=_params("parallel"),
        name="norm2_router",
    )(xs, norm_g, mod, mod, router_w, sh_gate_bf, sh_up_bf)


def _cast_weight(w_ref, wb_ref):
    @pl.when(pl.program_id(1) == 0)
    def _():
        wb_ref[...] = w_ref[...].astype(BF16)


def _proj_kernel(x_ref, w_ref, o_ref, wb_ref):
    _cast_weight(w_ref, wb_ref)
    o_ref[...] = jnp.dot(x_ref[...], wb_ref[...], preferred_element_type=F32).astype(o_ref.dtype)


def _proj_rope_kernel(x_ref, w_ref, cos_ref, sin_ref, o_ref, wb_ref, *, rope_cols, rope_rows):
    _cast_weight(w_ref, wb_ref)
    acc = jnp.dot(x_ref[...], wb_ref[...], preferred_element_type=F32)
    do_rope = jnp.logical_and(pl.program_id(0) < rope_cols, pl.program_id(1) < rope_rows)

    @pl.when(do_rope)
    def _():
        c, s = cos_ref[...], sin_ref[...]
        for h in range(acc.shape[1] // HEAD_DIM):
            sl = slice(h * HEAD_DIM, (h + 1) * HEAD_DIM)
            xh = acc[:, sl]
            o_ref[:, sl] = (xh * c + pltpu.roll(xh, HEAD_DIM // 2, 1) * s).astype(o_ref.dtype)

    @pl.when(jnp.logical_not(do_rope))
    def _():
        o_ref[...] = acc.astype(o_ref.dtype)


def _proj_resid_kernel(x_ref, w_ref, res_ref, gate_ref, o_ref, wb_ref):
    _cast_weight(w_ref, wb_ref)
    acc = jnp.dot(x_ref[...], wb_ref[...], preferred_element_type=F32)
    o_ref[...] = res_ref[...] + gate_ref[...] * acc


def _proj_tiles(m, n, seq, n_align=0):
    return _tile(math.gcd(m, seq), 1024), _tile(math.gcd(n, n_align), 512)


def _proj(cfg, x, w, layer_j, rope=None):
    m, k = x.shape
    n = w.shape[2]
    tm, tn = _proj_tiles(m, n, cfg.seq, 0 if rope is None else rope[2])
    grid = (n // tn, m // tm)
    x_spec = pl.BlockSpec((tm, k), lambda j, i: (i, 0))
    w_spec = pl.BlockSpec((None, k, tn), lambda j, i: (layer_j, 0, j))
    o_spec = pl.BlockSpec((tm, tn), lambda j, i: (i, j))
    common = dict(
        grid=grid, out_specs=o_spec,
        out_shape=jax.ShapeDtypeStruct((m, n), BF16),
        scratch_shapes=[pltpu.VMEM((k, tn), BF16)],
        compiler_params=_params("arbitrary", "arbitrary"),
    )
    if rope is None:
        return pl.pallas_call(_proj_kernel, in_specs=[x_spec, w_spec], name="proj", **common)(x, w)
    cos_t, sin_t, rope_n = rope
    per_seq = cfg.seq // tm
    t_spec = pl.BlockSpec((tm, HEAD_DIM), lambda j, i: (i % per_seq, 0))
    kern = functools.partial(_proj_rope_kernel, rope_cols=rope_n // tn, rope_rows=cfg.t_lat // tm)
    return pl.pallas_call(kern, in_specs=[x_spec, w_spec, t_spec, t_spec],
                          name="proj_rope", **common)(x, w, cos_t, sin_t)


def _proj_resid(cfg, x, w, layer_j, res, mod, layer, gate_chunk):
    m, k = x.shape
    n = w.shape[2]
    tm, tn = _proj_tiles(m, n, cfg.seq)
    return pl.pallas_call(
        _proj_resid_kernel,
        grid=(n // tn, m // tm),
        in_specs=[
            pl.BlockSpec((tm, k), lambda j, i: (i, 0)),
            pl.BlockSpec((None, k, tn), lambda j, i: (layer_j, 0, j)),
            pl.BlockSpec((tm, tn), lambda j, i: (i, j)),
            _mod_spec(cfg, layer, gate_chunk, tm, tn, col_of=True),
        ],
        out_specs=pl.BlockSpec((tm, tn), lambda j, i: (i, j)),
        out_shape=jax.ShapeDtypeStruct((m, n), F32),
        scratch_shapes=[pltpu.VMEM((k, tn), BF16)],
        compiler_params=_params("arbitrary", "arbitrary"),
        name="proj_resid",
    )(x, w, res, mod)


_NT = (((1,), (1,)), ((), ()))


def _gqa_head_group(q_ref, k_parts, v_parts, valid, sink_ref, layer_j, kvh, o_ref):
    dh = HEAD_DIM
    rows = q_ref.shape[0]
    scale = dh ** -0.5
    heads = [kvh * KV_GROUP + g for g in range(KV_GROUP)]
    qg = jnp.concatenate([q_ref[:, h * dh:(h + 1) * dh] for h in heads], axis=0)
    scores = []
    for idx, kp in enumerate(k_parts):
        s = lax.dot_general(qg, kp, _NT, preferred_element_type=F32) * scale
        if idx == 0 and valid is not None:
            s = jnp.where(valid, s, MASK_VALUE)
        scores.append(s)
    row_head = lax.broadcasted_iota(I32, (KV_GROUP * rows, 1), 0) // rows
    sink_col = jnp.zeros((KV_GROUP * rows, 1), F32)
    for g, h in enumerate(heads):
        sink_col = jnp.where(row_head == g, sink_ref[layer_j, h], sink_col)
    m = sink_col
    for s in scores:
        m = jnp.maximum(m, jnp.max(s, axis=-1, keepdims=True))
    es = [jnp.exp(s - m) for s in scores]
    denom = jnp.exp(sink_col - m)
    for e in es:
        denom = denom + jnp.sum(e, axis=-1, keepdims=True)
    inv = 1.0 / denom
    o = None
    for e, vp in zip(es, v_parts):
        part = jnp.dot((e * inv).astype(BF16), vp, preferred_element_type=F32)
        o = part if o is None else o + part
    for g, h in enumerate(heads):
        o_ref[:, h * dh:(h + 1) * dh] = o[g * rows:(g + 1) * rows].astype(o_ref.dtype)


def _gqa_lat_kernel(sink_ref, q_ref, kp_ref, ko_ref, kn_ref, vp_ref, vo_ref, vn_ref,
                    kc_ref, vc_ref, o_ref, *, layer_j, n_kv, n_blk):
    n = pl.program_id(1)
    dh = HEAD_DIM
    shape = (KV_GROUP * BLOCK, 3 * BLOCK)
    qi = lax.broadcasted_iota(I32, shape, 0) % BLOCK
    c = lax.broadcasted_iota(I32, shape, 1)
    rel = (c - BLOCK) - qi
    kpos = n * BLOCK - BLOCK + c
    valid = (jnp.abs(rel) <= WINDOW) & (kpos >= 0) & (kpos < n_blk * BLOCK)
    for kvh in range(n_kv):
        sl = slice(kvh * dh, (kvh + 1) * dh)
        k_lat = jnp.concatenate([kp_ref[:, sl], ko_ref[:, sl], kn_ref[:, sl]], axis=0)
        v_lat = jnp.concatenate([vp_ref[:, sl], vo_ref[:, sl], vn_ref[:, sl]], axis=0)
        _gqa_head_group(q_ref, [k_lat, kc_ref[:, sl]], [v_lat, vc_ref[:, sl]],
                        valid, sink_ref, layer_j, kvh, o_ref)


def _gqa_ctx_kernel(sink_ref, q_ref, kc_ref, vc_ref, o_ref, *, layer_j, n_kv):
    dh = HEAD_DIM
    for kvh in range(n_kv):
        sl = slice(kvh * dh, (kvh + 1) * dh)
        _gqa_head_group(q_ref, [kc_ref[:, sl]], [vc_ref[:, sl]], None, sink_ref, layer_j, kvh, o_ref)


def _gqa_attention(cfg, qkv, sink, layer_j, need_ctx):
    t = qkv.shape[0]
    d = cfg.d
    dh = HEAD_DIM
    n_kv = d // dh // KV_GROUP
    nk = n_kv * dh
    n_blk = cfg.seq // BLOCK
    kcol, vcol = d // nk, d // nk + 1
    ctx_blk0 = cfg.t_lat // cfg.ctx_len
    smem = pl.BlockSpec(memory_space=pltpu.SMEM)

    def band(col, shift):
        return pl.BlockSpec(
            (BLOCK, nk),
            lambda b, n: (b * n_blk + jnp.clip(n + shift, 0, n_blk - 1), col))

    def ctx_spec(col):
        return pl.BlockSpec((cfg.ctx_len, nk), lambda b, n: (ctx_blk0 + b, col))

    o_lat = pl.pallas_call(
        functools.partial(_gqa_lat_kernel, layer_j=layer_j, n_kv=n_kv, n_blk=n_blk),
        grid=(cfg.batch, n_blk),
        in_specs=[smem, pl.BlockSpec((BLOCK, d), lambda b, n: (b * n_blk + n, 0)),
                  band(kcol, -1), band(kcol, 0), band(kcol, 1),
                  band(vcol, -1), band(vcol, 0), band(vcol, 1),
                  ctx_spec(kcol), ctx_spec(vcol)],
        out_specs=pl.BlockSpec((BLOCK, d), lambda b, n: (b * n_blk + n, 0)),
        out_shape=jax.ShapeDtypeStruct((cfg.t_lat, d), BF16),
        compiler_params=_params("parallel", "parallel"),
        name="gqa_latent",
    )(sink, qkv, qkv, qkv, qkv, qkv, qkv, qkv, qkv, qkv)
    if not need_ctx:
        return o_lat, None

    def ctx1(col):
        return pl.BlockSpec((cfg.ctx_len, nk), lambda b: (ctx_blk0 + b, col))

    o_ctx = pl.pallas_call(
        functools.partial(_gqa_ctx_kernel, layer_j=layer_j, n_kv=n_kv),
        grid=(cfg.batch,),
        in_specs=[smem, pl.BlockSpec((cfg.ctx_len, d), lambda b: (ctx_blk0 + b, 0)),
                  ctx1(kcol), ctx1(vcol)],
        out_specs=pl.BlockSpec((cfg.ctx_len, d), lambda b: (b, 0)),
        out_shape=jax.ShapeDtypeStruct((cfg.t_ctx, d), BF16),
        compiler_params=_params("parallel"),
        name="gqa_context",
    )(sink, qkv, qkv, qkv)
    return o_lat, o_ctx


def _diff_kernel(lam_ref, g_ref, q_ref, *refs, lam_init, n_parts):
    k_refs, v_refs, o_ref = refs[:n_parts], refs[n_parts:2 * n_parts], refs[2 * n_parts]
    dh = HEAD_DIM
    scale = dh ** -0.5
    lf = lam_ref[...]
    lam = (jnp.exp(jnp.sum(lf[0:1] * lf[1:2], keepdims=True))
           - jnp.exp(jnp.sum(lf[2:3] * lf[3:4], keepdims=True)) + lam_init)
    probs = []
    for c in range(2):
        qc = q_ref[:, c * dh:(c + 1) * dh]
        ss = [lax.dot_general(qc, k_ref[:, c * dh:(c + 1) * dh], _NT,
                              preferred_element_type=F32) * scale for k_ref in k_refs]
        m = None
        for s in ss:
            mx = jnp.max(s, axis=-1, keepdims=True)
            m = mx if m is None else jnp.maximum(m, mx)
        es = [jnp.exp(s - m) for s in ss]
        denom = None
        for e in es:
            sm = jnp.sum(e, axis=-1, keepdims=True)
            denom = sm if denom is None else denom + sm
        probs.append((es, 1.0 / denom))
    (e0, inv0), (e1, inv1) = probs
    inv1 = inv1 * lam
    o = None
    for p in range(n_parts):
        a = (e0[p] * inv0 - e1[p] * inv1).astype(BF16)
        part = jnp.dot(a, v_refs[p][...], preferred_element_type=F32)
        o = part if o is None else o + part
    y = o * lax.rsqrt(jnp.mean(o * o, axis=-1, keepdims=True) + NORM_EPS)
    o_ref[...] = ((y * g_ref[...]) * (1.0 - lam_init)).astype(o_ref.dtype)


def _diff_attention(cfg, qkv, lam_vecs, subln_g, layer_j, lam_init, need_ctx):
    d = cfg.d
    dh = HEAD_DIM
    hw = 2 * dh
    nh = d // hw
    tq = _tile(cfg.seq, 512)
    qpb = cfg.seq // tq
    ctx_blk0 = cfg.t_lat // cfg.ctx_len
    lam_spec3 = pl.BlockSpec((None, 4, dh), lambda b, h, i: (layer_j, 0, 0))
    g_spec3 = pl.BlockSpec((None, 1, hw), lambda b, h, i: (layer_j, 0, 0))
    o_lat = pl.pallas_call(
        functools.partial(_diff_kernel, lam_init=lam_init, n_parts=2),
        grid=(cfg.batch, nh, qpb),
        in_specs=[lam_spec3, g_spec3,
                  pl.BlockSpec((tq, hw), lambda b, h, i: (b * qpb + i, h)),
                  pl.BlockSpec((cfg.seq, hw), lambda b, h, i: (b, nh + h)),
                  pl.BlockSpec((cfg.ctx_len, hw), lambda b, h, i: (ctx_blk0 + b, nh + h)),
                  pl.BlockSpec((cfg.seq, hw), lambda b, h, i: (b, 2 * nh + h)),
                  pl.BlockSpec((cfg.ctx_len, hw), lambda b, h, i: (ctx_blk0 + b, 2 * nh + h))],
        out_specs=pl.BlockSpec((tq, hw), lambda b, h, i: (b * qpb + i, h)),
        out_shape=jax.ShapeDtypeStruct((cfg.t_lat, d), BF16),
        compiler_params=_params("parallel", "parallel", "arbitrary"),
        name="diff_latent",
    )(lam_vecs, subln_g, qkv, qkv, qkv, qkv, qkv)
    if not need_ctx:
        return o_lat, None
    o_ctx = pl.pallas_call(
        functools.partial(_diff_kernel, lam_init=lam_init, n_parts=1),
        grid=(cfg.batch, nh),
        in_specs=[pl.BlockSpec((None, 4, dh), lambda b, h: (layer_j, 0, 0)),
                  pl.BlockSpec((None, 1, hw), lambda b, h: (layer_j, 0, 0)),
                  pl.BlockSpec((cfg.ctx_len, hw), lambda b, h: (ctx_blk0 + b, h)),
                  pl.BlockSpec((cfg.ctx_len, hw), lambda b, h: (ctx_blk0 + b, nh + h)),
                  pl.BlockSpec((cfg.ctx_len, hw), lambda b, h: (ctx_blk0 + b, 2 * nh + h))],
        out_specs=pl.BlockSpec((cfg.ctx_len, hw), lambda b, h: (b, h)),
        out_shape=jax.ShapeDtypeStruct((cfg.t_ctx, d), BF16),
        compiler_params=_params("parallel", "parallel"),
        name="diff_context",
    )(lam_vecs, subln_g, qkv, qkv, qkv)
    return o_lat, o_ctx


def _conv_kernel(gb_ref, gc_ref, u_ref, gcp_ref, up_ref, gcn_ref, un_ref, w_ref, o_ref,
                 *, blocks_per_seq, lat_blocks):
    i = pl.program_id(1)
    rb = gc_ref.shape[0]
    v = gc_ref[...].astype(F32) * u_ref[...].astype(F32)
    pos = jnp.where(i < lat_blocks, i % blocks_per_seq, 0)
    last = jnp.where(i < lat_blocks, blocks_per_seq - 1, 0)
    prev_row = jnp.where(pos > 0, gcp_ref[7:8, :].astype(F32) * up_ref[7:8, :].astype(F32), 0.0)
    next_row = jnp.where(pos < last, gcn_ref[0:1, :].astype(F32) * un_ref[0:1, :].astype(F32), 0.0)
    row = lax.broadcasted_iota(I32, v.shape, 0)
    v_prev = jnp.where(row == 0, prev_row, pltpu.roll(v, 1, 0))
    v_next = jnp.where(row == rb - 1, next_row, pltpu.roll(v, rb - 1, 0))
    w = w_ref[...]
    z = w[0:1] * v_prev + w[1:2] * v + w[2:3] * v_next
    o_ref[...] = (gb_ref[...].astype(F32) * z).astype(o_ref.dtype)


def _conv_gate(cfg, proj, conv_w, layer_j):
    t = proj.shape[0]
    d = cfg.d
    rb = _tile(cfg.ctx_len, 256)
    cb = _tile(d, 1024)
    ncb = d // cb
    sub = rb // 8
    last8 = t // 8 - 1
    blocks_per_seq = cfg.seq // rb
    lat_blocks = cfg.t_lat // rb
    assert cfg.ctx_len == rb, "context sequences must be exactly one row block"

    def main(part):
        return pl.BlockSpec((rb, cb), lambda j, i: (i, part * ncb + j))

    def halo(part, nxt):
        if nxt:
            return pl.BlockSpec((8, cb), lambda j, i: (jnp.minimum((i + 1) * sub, last8), part * ncb + j))
        return pl.BlockSpec((8, cb), lambda j, i: (jnp.maximum(i * sub - 1, 0), part * ncb + j))

    return pl.pallas_call(
        functools.partial(_conv_kernel, blocks_per_seq=blocks_per_seq, lat_blocks=lat_blocks),
        grid=(ncb, t // rb),
        in_specs=[main(0), main(1), main(2), halo(1, False), halo(2, False),
                  halo(1, True), halo(2, True),
                  pl.BlockSpec((None, CONV_W, cb), lambda j, i: (layer_j, 0, j))],
        out_specs=pl.BlockSpec((rb, cb), lambda j, i: (i, j)),
        out_shape=jax.ShapeDtypeStruct((t, d), BF16),
        compiler_params=_params("parallel", "parallel"),
        name="conv_gate",
    )(proj, proj, proj, proj, proj, proj, proj, conv_w)


def _route_kernel(logit_ref, bias_ref, eidx_ref, wts_ref, rank_ref, cnt_ref, run_ref):
    @pl.when(pl.program_id(0) == 0)
    def _():
        run_ref[...] = jnp.zeros_like(run_ref)

    scores = jax.nn.sigmoid(logit_ref[...])
    biased = scores + bias_ref[...]
    rows, ne = scores.shape
    per_group = ne // N_GROUPS
    lane_i = lax.broadcasted_iota(I32, (rows, ne), 1)
    grp = lane_i // per_group
    lane = lane_i.astype(F32)
    neg = jnp.float32(-jnp.inf)
    gscores = []
    for g in range(N_GROUPS):
        in_g = grp == g
        vals = jnp.where(in_g, biased, neg)
        m1 = jnp.max(vals, axis=-1, keepdims=True)
        i1 = jnp.min(jnp.where(vals == m1, lane, float(ne)), axis=-1, keepdims=True)
        m2 = jnp.max(jnp.where(lane == i1, neg, vals), axis=-1, keepdims=True)
        gscores.append(m1 + m2)
    emask = jnp.zeros((rows, ne), jnp.bool_)
    for g in range(N_GROUPS):
        beaten = jnp.zeros((rows, 1), I32)
        for h in range(N_GROUPS):
            if h == g:
                continue
            wins = (gscores[h] > gscores[g]) | ((gscores[h] == gscores[g]) & (h < g))
            beaten = beaten + wins.astype(I32)
        emask = emask | ((grp == g) & (beaten < TOPK_GROUPS))
    cand = jnp.where(emask, biased, MASK_VALUE)
    k_lane = lax.broadcasted_iota(I32, (rows, TOP_K), 1)
    eidx = jnp.zeros((rows, TOP_K), I32)
    wsel = jnp.zeros((rows, TOP_K), F32)
    hits = []
    for k in range(TOP_K):
        m = jnp.max(cand, axis=-1, keepdims=True)
        idx = jnp.min(jnp.where(cand == m, lane, float(ne)), axis=-1, keepdims=True)
        hit = lane == idx
        sc = jnp.sum(jnp.where(hit, scores, 0.0), axis=-1, keepdims=True)
        eidx = jnp.where(k_lane == k, idx.astype(I32), eidx)
        wsel = jnp.where(k_lane == k, sc, wsel)
        hits.append(hit)
        cand = jnp.where(hit, neg, cand)
    total = jnp.sum(wsel, axis=-1, keepdims=True)
    eidx_ref[...] = eidx
    wts_ref[...] = wsel / total * ROUTED_SCALE
    chosen = hits[0]
    for hit in hits[1:]:
        chosen = chosen | hit
    chosen_f = jnp.where(chosen, 1.0, 0.0)
    earlier = (lax.broadcasted_iota(I32, (rows, rows), 1)
               < lax.broadcasted_iota(I32, (rows, rows), 0))
    local = jnp.dot(jnp.where(earlier, 1.0, 0.0).astype(BF16), chosen_f.astype(BF16),
                    preferred_element_type=F32)
    rank_dense = local + run_ref[...]
    rank = jnp.zeros((rows, TOP_K), I32)
    for k, hit in enumerate(hits):
        rk = jnp.sum(jnp.where(hit, rank_dense, 0.0), axis=-1, keepdims=True)
        rank = jnp.where(k_lane == k, rk.astype(I32), rank)
    rank_ref[...] = rank
    run_ref[...] = run_ref[...] + jnp.sum(chosen_f, axis=0, keepdims=True)
    cnt_ref[...] = run_ref[...]


def _route(logits, router_b, layer):
    t, ne = logits.shape
    rb = _tile(t, 256)
    return pl.pallas_call(
        _route_kernel,
        grid=(t // rb,),
        in_specs=[pl.BlockSpec((rb, ne), lambda i: (i, 0)),
                  pl.BlockSpec((None, 1, ne), lambda i: (layer, 0, 0))],
        out_specs=[pl.BlockSpec((rb, TOP_K), lambda i: (i, 0)),
                   pl.BlockSpec((rb, TOP_K), lambda i: (i, 0)),
                   pl.BlockSpec((rb, TOP_K), lambda i: (i, 0)),
                   pl.BlockSpec((1, ne), lambda i: (0, 0))],
        out_shape=[jax.ShapeDtypeStruct((t, TOP_K), I32),
                   jax.ShapeDtypeStruct((t, TOP_K), F32),
                   jax.ShapeDtypeStruct((t, TOP_K), I32),
                   jax.ShapeDtypeStruct((1, ne), F32)],
        scratch_shapes=[pltpu.VMEM((1, ne), F32)],
        compiler_params=_params("arbitrary"),
        name="route",
    )(logits, router_b)


def _dispatch_plan(eidx, rank, counts_f, tile):
    ne = counts_f.shape[1]
    n_rows = eidx.shape[0] * TOP_K
    n_tiles = n_rows // tile
    n_work = n_tiles + ne
    counts = counts_f.reshape(ne).astype(I32)
    ends = jnp.cumsum(counts)
    offs = ends - counts
    e_ids = jnp.arange(ne, dtype=I32)
    pos = rank + jnp.sum(jnp.where(eidx[..., None] == e_ids, offs, 0), axis=-1)
    first_tile = offs // tile
    last_tile = jnp.maximum(ends - 1, offs) // tile
    n_items = jnp.where(counts > 0, last_tile - first_tile + 1, 0)
    item_end = jnp.cumsum(n_items)
    item_off = item_end - n_items
    total = item_end[-1]
    w = jnp.arange(n_work, dtype=I32)
    e_of = jnp.minimum(jnp.sum((item_end[None, :] <= w[:, None]).astype(I32), axis=1), ne - 1)
    last_e = jnp.max(jnp.where(counts > 0, e_ids, 0)).astype(I32)
    live = w < total
    e_of = jnp.where(live, e_of, last_e)
    sel = e_of[:, None] == e_ids

    def at_e(table):
        return jnp.sum(jnp.where(sel, table, 0), axis=1)

    tile_of = jnp.where(live, at_e(first_tile) + (w - at_e(item_off)), n_tiles - 1).astype(I32)
    lo = jnp.clip(at_e(offs) - tile_of * tile, 0, tile)
    hi = jnp.clip(at_e(ends) - tile_of * tile, 0, tile)
    lo = jnp.where(live, lo, 0).astype(I32)
    hi = jnp.where(live, hi, 0).astype(I32)
    prev_tile = jnp.concatenate([jnp.full((1,), -1, I32), tile_of[:-1]])
    prev_e = jnp.concatenate([jnp.full((1,), -1, I32), e_of[:-1]])
    new_tile = (tile_of != prev_tile).astype(I32)
    new_e = (e_of != prev_e).astype(I32)
    return pos, (tile_of, e_of, lo, hi, new_tile, new_e)


def _dispatch_kernel(pos_ref, f_ref, xs_ref, sem):
    rows = f_ref.shape[0]

    def issue(r, carry):
        for k in range(TOP_K):
            p = pos_ref[0, r * TOP_K + k]
            pltpu.make_async_copy(f_ref.at[pl.ds(r, 1)], xs_ref.at[pl.ds(p, 1)],
                                  sem).start(priority=k % 2)
        return carry

    lax.fori_loop(0, rows, issue, 0)
    for k in range(TOP_K):
        pltpu.make_async_copy(f_ref, xs_ref.at[pl.ds(0, rows)], sem).wait()


def _dispatch(fp, pos):
    t, dw = fp.shape
    rb = _tile(t, 256)
    nb = t // rb
    return pl.pallas_call(
        _dispatch_kernel,
        grid=(nb,),
        in_specs=[pl.BlockSpec((None, 1, rb * TOP_K), lambda i: (i, 0, 0), memory_space=pltpu.SMEM),
                  pl.BlockSpec((rb, dw), lambda i: (i, 0))],
        out_specs=pl.BlockSpec(memory_space=pl.ANY),
        out_shape=jax.ShapeDtypeStruct((t * TOP_K, dw), U32),
        scratch_shapes=[pltpu.SemaphoreType.DMA(())],
        compiler_params=_params("arbitrary"),
        name="moe_dispatch",
    )(pos.reshape(nb, 1, rb * TOP_K), fp)


def _row_mask(lo_ref, hi_ref, w, rows):
    r = lax.broadcasted_iota(I32, (rows, 1), 0)
    return (r >= lo_ref[w]) & (r < hi_ref[w])


def _gmm_up_kernel(tile_ref, e_ref, lo_ref, hi_ref, nt_ref, ne_ref,
                   x_ref, wgt_ref, wut_ref, o_ref, wcat_ref):
    w = pl.program_id(0)
    ff = wgt_ref.shape[0]

    @pl.when(ne_ref[w] == 1)
    def _():
        wcat_ref[0:ff] = wgt_ref[...].astype(BF16)
        wcat_ref[ff:2 * ff] = wut_ref[...].astype(BF16)

    lo, hi = _unpack_bf16_pair(x_ref[...])
    lo, hi = lo.astype(BF16), hi.astype(BF16)
    rows, half = lo.shape
    ht = (lax.dot_general(wcat_ref[:, :half], lo, _NT, preferred_element_type=F32)
          + lax.dot_general(wcat_ref[:, half:], hi, _NT, preferred_element_type=F32))
    act_t = _silu(ht[:ff]) * ht[ff:]
    pad = (-ff) % LANE
    if pad:
        act_t = jnp.concatenate([act_t, jnp.zeros((pad, rows), F32)], axis=0)
    act = act_t.T[:, :ff].astype(o_ref.dtype)
    m = _row_mask(lo_ref, hi_ref, w, act.shape[0])

    @pl.when(nt_ref[w] == 1)
    def _():
        o_ref[...] = jnp.where(m, act, jnp.zeros_like(act))

    @pl.when(nt_ref[w] == 0)
    def _():
        o_ref[...] = jnp.where(m, act, o_ref[...])


def _gmm_down_kernel(tile_ref, e_ref, lo_ref, hi_ref, nt_ref, ne_ref,
                     h_ref, wd_ref, o_ref, wdb_ref):
    w = pl.program_id(0)

    @pl.when(ne_ref[w] == 1)
    def _():
        wdb_ref[...] = wd_ref[...].astype(BF16)

    y = jnp.dot(h_ref[...], wdb_ref[...], preferred_element_type=F32)
    half = y.shape[1] // 2
    yp = _pack_bf16_pair(y[:, :half], y[:, half:])
    m = _row_mask(lo_ref, hi_ref, w, yp.shape[0])

    @pl.when(nt_ref[w] == 1)
    def _():
        o_ref[...] = jnp.where(m, yp, jnp.zeros_like(yp))

    @pl.when(nt_ref[w] == 0)
    def _():
        o_ref[...] = jnp.where(m, yp, o_ref[...])


def _gmm_up(xs_sorted, plan, exp_gate_t, exp_up_t, layer):
    n_rows, dw = xs_sorted.shape
    ff, d = exp_gate_t.shape[2], exp_gate_t.shape[3]
    n_work = plan[0].shape[0]
    w_spec = pl.BlockSpec((None, None, ff, d), lambda w, tl, e, *_: (layer, e[w], 0, 0))
    return pl.pallas_call(
        _gmm_up_kernel,
        grid_spec=pltpu.PrefetchScalarGridSpec(
            num_scalar_prefetch=6,
            grid=(n_work,),
            in_specs=[pl.BlockSpec((GMM_TILE, dw), lambda w, tl, *_: (tl[w], 0)), w_spec, w_spec],
            out_specs=pl.BlockSpec((GMM_TILE, ff), lambda w, tl, *_: (tl[w], 0)),
            scratch_shapes=[pltpu.VMEM((2 * ff, d), BF16)],
        ),
        out_shape=jax.ShapeDtypeStruct((n_rows, ff), BF16),
        compiler_params=_params("arbitrary"),
        name="moe_up",
    )(*plan, xs_sorted, exp_gate_t, exp_up_t)


def _gmm_down(h_sorted, plan, exp_down, layer):
    n_rows, ff = h_sorted.shape
    d = exp_down.shape[3]
    n_work = plan[0].shape[0]
    return pl.pallas_call(
        _gmm_down_kernel,
        grid_spec=pltpu.PrefetchScalarGridSpec(
            num_scalar_prefetch=6,
            grid=(n_work,),
            in_specs=[pl.BlockSpec((GMM_TILE, ff), lambda w, tl, *_: (tl[w], 0)),
                      pl.BlockSpec((None, None, ff, d), lambda w, tl, e, *_: (layer, e[w], 0, 0))],
            out_specs=pl.BlockSpec((GMM_TILE, d // 2), lambda w, tl, *_: (tl[w], 0)),
            scratch_shapes=[pltpu.VMEM((ff, d), BF16)],
        ),
        out_shape=jax.ShapeDtypeStruct((n_rows, d // 2), U32),
        compiler_params=_params("arbitrary"),
        name="moe_down",
    )(*plan, h_sorted, exp_down)


def _combine_gather(pos_ref, ys_ref, buf_ref, sem, slot):
    rows = buf_ref.shape[2]

    def issue(r, carry):
        for k in range(TOP_K):
            p = pos_ref[0, r * TOP_K + k]
            pltpu.make_async_copy(ys_ref.at[pl.ds(p, 1)], buf_ref.at[slot, k, pl.ds(r, 1)],
                                  sem.at[slot]).start(priority=k % 2)
        return carry

    lax.fori_loop(0, rows, issue, 0)


def _combine_moe(pos_ref, posn_ref, wts_ref, ys_ref, hsh_ref, sd_ref, buf_ref, sem):
    i = pl.program_id(0)
    rows = buf_ref.shape[2]
    slot = i % 2

    @pl.when(i == 0)
    def _():
        _combine_gather(pos_ref, ys_ref, buf_ref, sem, 0)

    @pl.when(i + 1 < pl.num_programs(0))
    def _():
        _combine_gather(posn_ref, ys_ref, buf_ref, sem, 1 - slot)

    acc = jnp.dot(hsh_ref[...], sd_ref[...], preferred_element_type=F32)
    for k in range(TOP_K):
        pltpu.make_async_copy(ys_ref.at[pl.ds(0, rows)], buf_ref.at[slot, k], sem.at[slot]).wait()
    wts = wts_ref[...]
    for k in range(TOP_K):
        lo, hi = _unpack_bf16_pair(buf_ref[slot, k])
        acc = acc + wts[:, k:k + 1] * jnp.concatenate([lo, hi], axis=1)
    return acc


def _combine_kernel(pos_ref, posn_ref, wts_ref, ys_ref, x_ref, hsh_ref, sd_ref, g2_ref,
                    ng_ref, sc_ref, sh_ref, xo_ref, a_ref, buf_ref, sem):
    acc = _combine_moe(pos_ref, posn_ref, wts_ref, ys_ref, hsh_ref, sd_ref, buf_ref, sem)
    xn = x_ref[...] + g2_ref[...] * acc
    xo_ref[...] = xn
    a_ref[...] = _norm_mod(xn, ng_ref[...], sc_ref[...], sh_ref[...]).astype(a_ref.dtype)


def _combine_final_kernel(pos_ref, posn_ref, wts_ref, ys_ref, x_ref, hsh_ref, sd_ref, g2_ref,
                          ng_ref, o_ref, buf_ref, sem):
    acc = _combine_moe(pos_ref, posn_ref, wts_ref, ys_ref, hsh_ref, sd_ref, buf_ref, sem)
    xn = x_ref[...] + g2_ref[...] * acc
    y = xn * lax.rsqrt(jnp.mean(xn * xn, axis=-1, keepdims=True) + NORM_EPS)
    o_ref[...] = y * ng_ref[...]


def _combine(cfg, xs, ys_sorted, pos, wts, hsh, sh_down_bf, mod, layer, next_g, final):
    t, d = xs.shape
    ff = hsh.shape[1]
    rb = _tile(cfg.ctx_len, 128)
    nb = t // rb
    pos3 = pos.reshape(nb, 1, rb * TOP_K)
    row_spec = pl.BlockSpec((rb, d), lambda i: (i, 0))
    in_specs = [pl.BlockSpec((None, 1, rb * TOP_K), lambda i: (i, 0, 0), memory_space=pltpu.SMEM),
                pl.BlockSpec((None, 1, rb * TOP_K), lambda i: (jnp.minimum(i + 1, nb - 1), 0, 0),
                             memory_space=pltpu.SMEM),
                pl.BlockSpec((rb, TOP_K), lambda i: (i, 0)),
                pl.BlockSpec(memory_space=pl.ANY),
                row_spec,
                pl.BlockSpec((rb, ff), lambda i: (i, 0)),
                pl.BlockSpec((None, ff, d), lambda i: (layer, 0, 0)),
                _mod_spec(cfg, layer, 5, rb, d)]
    scratch = [pltpu.VMEM((2, TOP_K, rb, d // 2), U32), pltpu.SemaphoreType.DMA((2,))]
    args = (pos3, pos3, wts, ys_sorted, xs, hsh, sh_down_bf, mod)
    if final:
        return pl.pallas_call(
            _combine_final_kernel,
            grid=(nb,),
            in_specs=in_specs + [pl.BlockSpec((1, d), lambda i: (0, 0))],
            out_specs=row_spec,
            out_shape=jax.ShapeDtypeStruct((t, d), F32),
            scratch_shapes=scratch,
            compiler_params=_params("arbitrary"),
            name="moe_combine_final",
        )(*args, next_g)
    nxt = layer + 1
    return pl.pallas_call(
        _combine_kernel,
        grid=(nb,),
        in_specs=in_specs + [pl.BlockSpec((None, 1, d), lambda i: (nxt, 0, 0)),
                             _mod_spec(cfg, nxt, 1, rb, d), _mod_spec(cfg, nxt, 0, rb, d)],
        out_specs=[row_spec, row_spec],
        out_shape=[jax.ShapeDtypeStruct((t, d), F32), jax.ShapeDtypeStruct((t, d), BF16)],
        scratch_shapes=scratch,
        compiler_params=_params("arbitrary"),
        name="moe_combine",
    )(*args, next_g, mod, mod)


def _rope_tables(seq):
    rows = seq // GRID_W
    row = jnp.broadcast_to(jnp.arange(rows)[:, None], (rows, GRID_W)).reshape(-1).astype(F32)
    col = jnp.broadcast_to(jnp.arange(GRID_W)[None, :], (rows, GRID_W)).reshape(-1).astype(F32)
    n_freq = HEAD_DIM // 4
    inv = ROPE_THETA ** (-jnp.arange(n_freq, dtype=F32) / n_freq)
    ang = jnp.concatenate([row[:, None] * inv, col[:, None] * inv], axis=-1)
    cos, sin = jnp.cos(ang), jnp.sin(ang)
    return jnp.concatenate([cos, cos], axis=-1), jnp.concatenate([-sin, sin], axis=-1)


def kernel(x, c, ctx, c_ctx, ada_down, ada_up, ada_b, norm1_g, norm2_g, a_w_qkv, a_w_o, a_sink,
           b_w_qkv, b_w_o, b_lam, b_subln_g, c_w_in, c_conv, c_w_out, router_w, router_b,
           exp_gate, exp_up, exp_down, sh_gate, sh_up, sh_down, final_g):
    batch, seq, d = x.shape
    ctx_len = ctx.shape[1]
    depth = ada_down.shape[0]
    cfg = _Cfg(batch, seq, ctx_len, d)
    assert batch + 1 <= MOD_ROWS

    xs = jnp.concatenate([x.reshape(-1, d), ctx.reshape(-1, d)], axis=0)
    src = jnp.concatenate([c, c_ctx[None, :], jnp.zeros((MOD_ROWS - batch - 1, d), F32)], axis=0)
    mod = _ada_table(src, ada_down, ada_up, ada_b).reshape(depth * MOD_ROWS, 1, 6 * d)
    cos_t, sin_t = _rope_tables(seq)
    norm1_g3 = norm1_g.reshape(depth, 1, d)
    norm2_g3 = norm2_g.reshape(depth, 1, d)
    router_b3 = router_b.reshape(depth, 1, -1)
    b_subln_g3 = b_subln_g.reshape(b_subln_g.shape[0], 1, -1)
    sh_gate_bf, sh_up_bf, sh_down_bf = (w.astype(BF16) for w in (sh_gate, sh_up, sh_down))
    router_hi = router_w.astype(BF16)
    router_lo = (router_w - router_hi.astype(F32)).astype(BF16)
    router_hl = jnp.concatenate([router_hi, router_lo], axis=-1)
    exp_gate_t, exp_up_t = jnp.swapaxes(exp_gate, 2, 3), jnp.swapaxes(exp_up, 2, 3)

    a = _norm1(cfg, xs, norm1_g3, mod, 0)
    out = None
    for i in range(depth):
        kind, j = i % N_MIXERS, i // N_MIXERS
        need_ctx = i < depth - 1
        if kind == 0:
            nq = d
            nk = d // KV_GROUP
            qkv = _proj(cfg, a, a_w_qkv, j, rope=(cos_t, sin_t, nq + nk))
            o_lat, o_ctx = _gqa_attention(cfg, qkv, a_sink, j, need_ctx)
            w_o = a_w_o
        elif kind == 1:
            lam_init = 0.8 - 0.6 * math.exp(-0.3 * i)
            qkv = _proj(cfg, a, b_w_qkv, j, rope=(cos_t, sin_t, 2 * d))
            o_lat, o_ctx = _diff_attention(cfg, qkv, b_lam, b_subln_g3, j, lam_init, need_ctx)
            w_o = b_w_o
        else:
            pr = _proj(cfg, a, c_w_in, j)
            o = _conv_gate(cfg, pr, c_conv, j)
            w_o = c_w_out
        if kind != 2:
            o = o_lat if o_ctx is None else jnp.concatenate([o_lat, o_ctx], axis=0)
        elif not need_ctx:
            o = o[:cfg.t_lat]
        xs = _proj_resid(cfg, o, w_o, j, xs, mod, i, 2)

        fp, logits, hsh = _norm2(cfg, xs, norm2_g3, mod, i, router_hl, sh_gate_bf, sh_up_bf)
        eidx, wts, rank, counts = _route(logits, router_b3, i)
        pos, plan = _dispatch_plan(eidx, rank, counts, GMM_TILE)
        xs_sorted = _dispatch(fp, pos)
        h_sorted = _gmm_up(xs_sorted, plan, exp_gate_t, exp_up_t, i)
        ys_sorted = _gmm_down(h_sorted, plan, exp_down, i)
        if i == depth - 1:
            out = _combine(cfg, xs, ys_sorted, pos, wts, hsh, sh_down_bf, mod, i,
                           final_g.reshape(1, d), True)
        else:
            xs, a = _combine(cfg, xs, ys_sorted, pos, wts, hsh, sh_down_bf, mod, i, norm1_g3, False)
    return out.reshape(batch, seq, d)
```

```python
import functools
import math

import jax
import jax.numpy as jnp
from jax import lax
from jax.experimental import pallas as pl
from jax.experimental.pallas import tpu as pltpu

F32 = jnp.float32
BF16 = jnp.bfloat16
U32 = jnp.uint32
I32 = jnp.int32

HEAD_DIM = 128
GRID_W = 64
ROPE_THETA = 10000.0
NORM_EPS = 1e-6
MASK_VALUE = -1e30
WINDOW = 128
BLOCK = 128
CONV_W = 3
N_MIXERS = 3
N_EXPERTS = 64
TOP_K = 8
N_GROUPS = 8
TOPK_GROUPS = 4
ROUTED_SCALE = 2.5
KV_GROUP = 4
MOD_ROWS = 8

LANE = 128
SUBLANES = 8
VMEM_LIMIT_BYTES = 56 * 1024 * 1024
GMM_TILE = 256
HI_MASK = 0xFFFF0000
LOG2E = 1.4426950408889634
QUERY_SCALE = HEAD_DIM ** -0.5 * LOG2E


def _params(*sem):
    return pltpu.CompilerParams(dimension_semantics=sem, vmem_limit_bytes=VMEM_LIMIT_BYTES)


def _tile(n, pref):
    t = min(n, pref)
    while n % t:
        t //= 2
    return t


def _silu(v):
    return v * jax.nn.sigmoid(v)


def _pack_bf16_pair(lo, hi):
    lo_b = lax.bitcast_convert_type(lo.astype(BF16).astype(F32), U32)
    hi_b = lax.bitcast_convert_type(hi.astype(BF16).astype(F32), U32)
    return (lo_b >> 16) | (hi_b & jnp.uint32(HI_MASK))


def _unpack_bf16_pair(u):
    lo = lax.bitcast_convert_type(u << 16, F32)
    hi = lax.bitcast_convert_type(u & jnp.uint32(HI_MASK), F32)
    return lo, hi


def _ada_kernel(src_ref, down_ref, up_ref, b_ref, out_ref, t1_ref):
    @pl.when(pl.program_id(1) == 0)
    def _():
        s = _silu(src_ref[...]).astype(BF16)
        t1 = jnp.dot(s, down_ref[...].astype(BF16), preferred_element_type=F32)
        t1_ref[...] = t1.astype(BF16)

    out_ref[...] = jnp.dot(t1_ref[...], up_ref[...].astype(BF16),
                           preferred_element_type=F32) + b_ref[...]


def _ada_table(src, ada_down, ada_up, ada_b):
    depth, d, rank = ada_down.shape
    n6 = ada_up.shape[2]
    tn = _tile(n6, 2048)
    return pl.pallas_call(
        _ada_kernel,
        grid=(depth, n6 // tn),
        in_specs=[
            pl.BlockSpec((MOD_ROWS, d), lambda l, j: (0, 0)),
            pl.BlockSpec((None, d, rank), lambda l, j: (l, 0, 0)),
            pl.BlockSpec((None, rank, tn), lambda l, j: (l, 0, j)),
            pl.BlockSpec((None, 1, tn), lambda l, j: (l, 0, j)),
        ],
        out_specs=pl.BlockSpec((None, MOD_ROWS, tn), lambda l, j: (l, 0, j)),
        out_shape=jax.ShapeDtypeStruct((depth, MOD_ROWS, n6), F32),
        scratch_shapes=[pltpu.VMEM((MOD_ROWS, rank), BF16)],
        compiler_params=_params("arbitrary", "arbitrary"),
        name="ada_table",
    )(src, ada_down, ada_up, ada_b.reshape(depth, 1, n6))


def _norm_mod(xf, g, sc, sh):
    y = xf * lax.rsqrt(jnp.mean(xf * xf, axis=-1, keepdims=True) + NORM_EPS)
    return (y * g) * (1.0 + sc) + sh


class _Cfg:
    def __init__(self, batch, seq, ctx_len, d):
        self.batch, self.seq, self.ctx_len, self.d = batch, seq, ctx_len, d
        self.t_lat = batch * seq
        self.t_ctx = batch * ctx_len
        self.t = self.t_lat + self.t_ctx

    def group_of_rows(self, rows_per_block):
        per_seq = self.seq // rows_per_block
        assert per_seq * rows_per_block == self.seq
        assert self.t_lat % rows_per_block == 0 and self.t_ctx % rows_per_block == 0
        batch = self.batch
        return lambda i: jnp.minimum(i // per_seq, batch)


def _mod_spec(cfg, layer, chunk, rows_per_block, width, col_of=None):
    grp = cfg.group_of_rows(rows_per_block)
    per_chunk = cfg.d // width
    if col_of is None:
        return pl.BlockSpec((None, 1, width),
                            lambda i: (layer * MOD_ROWS + grp(i), 0, chunk * per_chunk))
    return pl.BlockSpec((None, 1, width),
                        lambda j, i: (layer * MOD_ROWS + grp(i), 0, chunk * per_chunk + j))


def _norm1_kernel(x_ref, g_ref, sc_ref, sh_ref, a_ref):
    a_ref[...] = _norm_mod(x_ref[...], g_ref[...], sc_ref[...], sh_ref[...]).astype(BF16)


def _norm1(cfg, xs, norm_g, mod, layer):
    t, d = xs.shape
    rb = _tile(cfg.ctx_len, 256)
    return pl.pallas_call(
        _norm1_kernel,
        grid=(t // rb,),
        in_specs=[
            pl.BlockSpec((rb, d), lambda i: (i, 0)),
            pl.BlockSpec((None, 1, d), lambda i: (layer, 0, 0)),
            _mod_spec(cfg, layer, 1, rb, d),
            _mod_spec(cfg, layer, 0, rb, d),
        ],
        out_specs=pl.BlockSpec((rb, d), lambda i: (i, 0)),
        out_shape=jax.ShapeDtypeStruct((t, d), BF16),
        compiler_params=_params("parallel"),
        name="norm1",
    )(xs, norm_g, mod, mod)


def _norm2_kernel(x_ref, g_ref, sc_ref, sh_ref, rw_ref, sg_ref, su_ref,
                  fp_ref, logit_ref, hsh_ref):
    f = _norm_mod(x_ref[...], g_ref[...], sc_ref[...], sh_ref[...])
    half = f.shape[1] // 2
    fp_ref[...] = _pack_bf16_pair(f[:, :half], f[:, half:])
    fb = f.astype(BF16)
    f_lo = (f - fb.astype(F32)).astype(BF16)
    ne = logit_ref.shape[0]
    both = (jnp.dot(fb, rw_ref[...], preferred_element_type=F32)
            + jnp.dot(f_lo, rw_ref[...], preferred_element_type=F32))
    both_t = both.T
    logit_ref[...] = both_t[:ne] + both_t[ne:]
    hg = jnp.dot(fb, sg_ref[...], preferred_element_type=F32)
    hu = jnp.dot(fb, su_ref[...], preferred_element_type=F32)
    hsh_ref[...] = (_silu(hg) * hu).astype(BF16)


def _norm2(cfg, xs, norm_g, mod, layer, router_w, sh_gate_bf, sh_up_bf):
    t, d = xs.shape
    ne = router_w.shape[2] // 2
    ff = sh_gate_bf.shape[2]
    rb = _tile(cfg.ctx_len, 256)
    return pl.pallas_call(
        _norm2_kernel,
        grid=(t // rb,),
        in_specs=[
            pl.BlockSpec((rb, d), lambda i: (i, 0)),
            pl.BlockSpec((None, 1, d), lambda i: (layer, 0, 0)),
            _mod_spec(cfg, layer, 4, rb, d),
            _mod_spec(cfg, layer, 3, rb, d),
            pl.BlockSpec((None, d, 2 * ne), lambda i: (layer, 0, 0)),
            pl.BlockSpec((None, d, ff), lambda i: (layer, 0, 0)),
            pl.BlockSpec((None, d, ff), lambda i: (layer, 0, 0)),
        ],
        out_specs=[
            pl.BlockSpec((rb, d // 2), lambda i: (i, 0)),
            pl.BlockSpec((ne, rb), lambda i: (0, i)),
            pl.BlockSpec((rb, ff), lambda i: (i, 0)),
        ],
        out_shape=[
            jax.ShapeDtypeStruct((t, d // 2), U32),
            jax.ShapeDtypeStruct((ne, t), F32),
            jax.ShapeDtypeStruct((t, ff), BF16),
        ],
        compiler_params=_params("parallel"),
        name="norm2_router",
    )(xs, norm_g, mod, mod, router_w, sh_gate_bf, sh_up_bf)


def _cast_weight(w_ref, wb_ref):
    @pl.when(pl.program_id(1) == 0)
    def _():
        wb_ref[...] = w_ref[...].astype(BF16)


def _proj_kernel(x_ref, w_ref, o_ref, wb_ref):
    _cast_weight(w_ref, wb_ref)
    o_ref[...] = jnp.dot(x_ref[...], wb_ref[...], preferred_element_type=F32).astype(o_ref.dtype)


def _proj_rope_kernel(x_ref, w_ref, cos_ref, sin_ref, o_ref, wb_ref, *, rope_cols, rope_rows, q_cols):
    _cast_weight(w_ref, wb_ref)
    acc = jnp.dot(x_ref[...], wb_ref[...], preferred_element_type=F32)
    acc = acc * jnp.where(pl.program_id(0) < q_cols, QUERY_SCALE, 1.0)
    do_rope = jnp.logical_and(pl.program_id(0) < rope_cols, pl.program_id(1) < rope_rows)

    @pl.when(do_rope)
    def _():
        c, s = cos_ref[...], sin_ref[...]
        for h in range(acc.shape[1] // HEAD_DIM):
            sl = slice(h * HEAD_DIM, (h + 1) * HEAD_DIM)
            xh = acc[:, sl]
            o_ref[:, sl] = (xh * c + pltpu.roll(xh, HEAD_DIM // 2, 1) * s).astype(o_ref.dtype)

    @pl.when(jnp.logical_not(do_rope))
    def _():
        o_ref[...] = acc.astype(o_ref.dtype)


def _proj_resid_kernel(x_ref, w_ref, res_ref, gate_ref, o_ref, wb_ref):
    _cast_weight(w_ref, wb_ref)
    acc = jnp.dot(x_ref[...], wb_ref[...], preferred_element_type=F32)
    o_ref[...] = res_ref[...] + gate_ref[...] * acc


def _proj_tiles(m, n, seq, n_align=0):
    return _tile(math.gcd(m, seq), 1024), _tile(math.gcd(n, n_align), 512)


def _proj(cfg, x, w, layer_j, rope=None):
    m, k = x.shape
    n = w.shape[2]
    tm, tn = _proj_tiles(m, n, cfg.seq, 0 if rope is None else rope[2])
    grid = (n // tn, m // tm)
    x_spec = pl.BlockSpec((tm, k), lambda j, i: (i, 0))
    w_spec = pl.BlockSpec((None, k, tn), lambda j, i: (layer_j, 0, j))
    o_spec = pl.BlockSpec((tm, tn), lambda j, i: (i, j))
    common = dict(
        grid=grid, out_specs=o_spec,
        out_shape=jax.ShapeDtypeStruct((m, n), BF16),
        scratch_shapes=[pltpu.VMEM((k, tn), BF16)],
        compiler_params=_params("arbitrary", "arbitrary"),
    )
    if rope is None:
        return pl.pallas_call(_proj_kernel, in_specs=[x_spec, w_spec], name="proj", **common)(x, w)
    cos_t, sin_t, rope_n = rope
    per_seq = cfg.seq // tm
    t_spec = pl.BlockSpec((tm, HEAD_DIM), lambda j, i: (i % per_seq, 0))
    assert cfg.d % tn == 0
    kern = functools.partial(_proj_rope_kernel, rope_cols=rope_n // tn, rope_rows=cfg.t_lat // tm,
                             q_cols=cfg.d // tn)
    return pl.pallas_call(kern, in_specs=[x_spec, w_spec, t_spec, t_spec],
                          name="proj_rope", **common)(x, w, cos_t, sin_t)


def _proj_resid(cfg, x, w, layer_j, res, mod, layer, gate_chunk):
    m, k = x.shape
    n = w.shape[2]
    tm, tn = _proj_tiles(m, n, cfg.seq)
    return pl.pallas_call(
        _proj_resid_kernel,
        grid=(n // tn, m // tm),
        in_specs=[
            pl.BlockSpec((tm, k), lambda j, i: (i, 0)),
            pl.BlockSpec((None, k, tn), lambda j, i: (layer_j, 0, j)),
            pl.BlockSpec((tm, tn), lambda j, i: (i, j)),
            _mod_spec(cfg, layer, gate_chunk, tm, tn, col_of=True),
        ],
        out_specs=pl.BlockSpec((tm, tn), lambda j, i: (i, j)),
        out_shape=jax.ShapeDtypeStruct((m, n), F32),
        scratch_shapes=[pltpu.VMEM((k, tn), BF16)],
        compiler_params=_params("arbitrary", "arbitrary"),
        name="proj_resid",
    )(x, w, res, mod)


_NT = (((1,), (1,)), ((), ()))


def _gqa_head_group(q_ref, k_parts, v_parts, valid, sink_ref, layer_j, kvh, o_ref):
    dh = HEAD_DIM
    rows = q_ref.shape[0]
    heads = [kvh * KV_GROUP + g for g in range(KV_GROUP)]
    qg = jnp.concatenate([q_ref[:, h * dh:(h + 1) * dh] for h in heads], axis=0)
    scores = []
    for idx, kp in enumerate(k_parts):
        s = lax.dot_general(qg, kp, _NT, preferred_element_type=F32)
        if idx == 0 and valid is not None:
            s = jnp.where(valid, s, MASK_VALUE)
        scores.append(s)
    row_head = lax.broadcasted_iota(I32, (KV_GROUP * rows, 1), 0) // rows
    sink_col = jnp.zeros((KV_GROUP * rows, 1), F32)
    for g, h in enumerate(heads):
        sink_col = jnp.where(row_head == g, sink_ref[layer_j, h] * LOG2E, sink_col)
    m = sink_col
    for s in scores:
        m = jnp.maximum(m, jnp.max(s, axis=-1, keepdims=True))
    es = [jnp.exp2(s - m) for s in scores]
    denom = jnp.exp2(sink_col - m)
    for e in es:
        denom = denom + jnp.sum(e, axis=-1, keepdims=True)
    o = None
    for e, vp in zip(es, v_parts):
        part = jnp.dot(e.astype(BF16), vp, preferred_element_type=F32)
        o = part if o is None else o + part
    o = o * (1.0 / denom)
    for g, h in enumerate(heads):
        o_ref[:, h * dh:(h + 1) * dh] = o[g * rows:(g + 1) * rows].astype(o_ref.dtype)


def _gqa_lat_kernel(sink_ref, q_ref, kp_ref, ko_ref, kn_ref, vp_ref, vo_ref, vn_ref,
                    kc_ref, vc_ref, o_ref, *, layer_j, n_kv, n_blk):
    n = pl.program_id(1)
    dh = HEAD_DIM
    shape = (KV_GROUP * BLOCK, 3 * BLOCK)
    qi = lax.broadcasted_iota(I32, shape, 0) % BLOCK
    c = lax.broadcasted_iota(I32, shape, 1)
    rel = (c - BLOCK) - qi
    kpos = n * BLOCK - BLOCK + c
    valid = (jnp.abs(rel) <= WINDOW) & (kpos >= 0) & (kpos < n_blk * BLOCK)
    for kvh in range(n_kv):
        sl = slice(kvh * dh, (kvh + 1) * dh)
        k_lat = jnp.concatenate([kp_ref[:, sl], ko_ref[:, sl], kn_ref[:, sl]], axis=0)
        v_lat = jnp.concatenate([vp_ref[:, sl], vo_ref[:, sl], vn_ref[:, sl]], axis=0)
        _gqa_head_group(q_ref, [k_lat, kc_ref[:, sl]], [v_lat, vc_ref[:, sl]],
                        valid, sink_ref, layer_j, kvh, o_ref)


def _gqa_ctx_kernel(sink_ref, q_ref, kc_ref, vc_ref, o_ref, *, layer_j, n_kv):
    dh = HEAD_DIM
    for kvh in range(n_kv):
        sl = slice(kvh * dh, (kvh + 1) * dh)
        _gqa_head_group(q_ref, [kc_ref[:, sl]], [vc_ref[:, sl]], None, sink_ref, layer_j, kvh, o_ref)


def _gqa_attention(cfg, qkv, sink, layer_j, need_ctx):
    t = qkv.shape[0]
    d = cfg.d
    dh = HEAD_DIM
    n_kv = d // dh // KV_GROUP
    nk = n_kv * dh
    n_blk = cfg.seq // BLOCK
    kcol, vcol = d // nk, d // nk + 1
    ctx_blk0 = cfg.t_lat // cfg.ctx_len
    smem = pl.BlockSpec(memory_space=pltpu.SMEM)

    def band(col, shift):
        return pl.BlockSpec(
            (BLOCK, nk),
            lambda b, n: (b * n_blk + jnp.clip(n + shift, 0, n_blk - 1), col))

    def ctx_spec(col):
        return pl.BlockSpec((cfg.ctx_len, nk), lambda b, n: (ctx_blk0 + b, col))

    o_lat = pl.pallas_call(
        functools.partial(_gqa_lat_kernel, layer_j=layer_j, n_kv=n_kv, n_blk=n_blk),
        grid=(cfg.batch, n_blk),
        in_specs=[smem, pl.BlockSpec((BLOCK, d), lambda b, n: (b * n_blk + n, 0)),
                  band(kcol, -1), band(kcol, 0), band(kcol, 1),
                  band(vcol, -1), band(vcol, 0), band(vcol, 1),
                  ctx_spec(kcol), ctx_spec(vcol)],
        out_specs=pl.BlockSpec((BLOCK, d), lambda b, n: (b * n_blk + n, 0)),
        out_shape=jax.ShapeDtypeStruct((cfg.t_lat, d), BF16),
        compiler_params=_params("parallel", "parallel"),
        name="gqa_latent",
    )(sink, qkv, qkv, qkv, qkv, qkv, qkv, qkv, qkv, qkv)
    if not need_ctx:
        return o_lat, None

    def ctx1(col):
        return pl.BlockSpec((cfg.ctx_len, nk), lambda b: (ctx_blk0 + b, col))

    o_ctx = pl.pallas_call(
        functools.partial(_gqa_ctx_kernel, layer_j=layer_j, n_kv=n_kv),
        grid=(cfg.batch,),
        in_specs=[smem, pl.BlockSpec((cfg.ctx_len, d), lambda b: (ctx_blk0 + b, 0)),
                  ctx1(kcol), ctx1(vcol)],
        out_specs=pl.BlockSpec((cfg.ctx_len, d), lambda b: (b, 0)),
        out_shape=jax.ShapeDtypeStruct((cfg.t_ctx, d), BF16),
        compiler_params=_params("parallel"),
        name="gqa_context",
    )(sink, qkv, qkv, qkv)
    return o_lat, o_ctx


def _diff_kernel(lam_ref, g_ref, q_ref, *refs, lam_init, n_parts):
    k_refs, v_refs, o_ref = refs[:n_parts], refs[n_parts:2 * n_parts], refs[2 * n_parts]
    dh = HEAD_DIM
    lf = lam_ref[...]
    lam = (jnp.exp(jnp.sum(lf[0:1] * lf[1:2], keepdims=True))
           - jnp.exp(jnp.sum(lf[2:3] * lf[3:4], keepdims=True)) + lam_init)
    outs = []
    for c in range(2):
        qc = q_ref[:, c * dh:(c + 1) * dh]
        ss = [lax.dot_general(qc, k_ref[:, c * dh:(c + 1) * dh], _NT,
                              preferred_element_type=F32) for k_ref in k_refs]
        m = None
        for s in ss:
            mx = jnp.max(s, axis=-1, keepdims=True)
            m = mx if m is None else jnp.maximum(m, mx)
        denom = None
        pv = None
        for s, v_ref in zip(ss, v_refs):
            e = jnp.exp2(s - m)
            sm = jnp.sum(e, axis=-1, keepdims=True)
            denom = sm if denom is None else denom + sm
            part = jnp.dot(e.astype(BF16), v_ref[...], preferred_element_type=F32)
            pv = part if pv is None else pv + part
        outs.append(pv * (1.0 / denom))
    o = outs[0] - lam * outs[1]
    y = o * lax.rsqrt(jnp.mean(o * o, axis=-1, keepdims=True) + NORM_EPS)
    o_ref[...] = ((y * g_ref[...]) * (1.0 - lam_init)).astype(o_ref.dtype)


def _diff_attention(cfg, qkv, lam_vecs, subln_g, layer_j, lam_init, need_ctx):
    d = cfg.d
    dh = HEAD_DIM
    hw = 2 * dh
    nh = d // hw
    tq = _tile(cfg.seq, 512)
    qpb = cfg.seq // tq
    ctx_blk0 = cfg.t_lat // cfg.ctx_len
    lam_spec3 = pl.BlockSpec((None, 4, dh), lambda b, h, i: (layer_j, 0, 0))
    g_spec3 = pl.BlockSpec((None, 1, hw), lambda b, h, i: (layer_j, 0, 0))
    o_lat = pl.pallas_call(
        functools.partial(_diff_kernel, lam_init=lam_init, n_parts=2),
        grid=(cfg.batch, nh, qpb),
        in_specs=[lam_spec3, g_spec3,
                  pl.BlockSpec((tq, hw), lambda b, h, i: (b * qpb + i, h)),
                  pl.BlockSpec((cfg.seq, hw), lambda b, h, i: (b, nh + h)),
                  pl.BlockSpec((cfg.ctx_len, hw), lambda b, h, i: (ctx_blk0 + b, nh + h)),
                  pl.BlockSpec((cfg.seq, hw), lambda b, h, i: (b, 2 * nh + h)),
                  pl.BlockSpec((cfg.ctx_len, hw), lambda b, h, i: (ctx_blk0 + b, 2 * nh + h))],
        out_specs=pl.BlockSpec((tq, hw), lambda b, h, i: (b * qpb + i, h)),
        out_shape=jax.ShapeDtypeStruct((cfg.t_lat, d), BF16),
        compiler_params=_params("parallel", "parallel", "arbitrary"),
        name="diff_latent",
    )(lam_vecs, subln_g, qkv, qkv, qkv, qkv, qkv)
    if not need_ctx:
        return o_lat, None
    o_ctx = pl.pallas_call(
        functools.partial(_diff_kernel, lam_init=lam_init, n_parts=1),
        grid=(cfg.batch, nh),
        in_specs=[pl.BlockSpec((None, 4, dh), lambda b, h: (layer_j, 0, 0)),
                  pl.BlockSpec((None, 1, hw), lambda b, h: (layer_j, 0, 0)),
                  pl.BlockSpec((cfg.ctx_len, hw), lambda b, h: (ctx_blk0 + b, h)),
                  pl.BlockSpec((cfg.ctx_len, hw), lambda b, h: (ctx_blk0 + b, nh + h)),
                  pl.BlockSpec((cfg.ctx_len, hw), lambda b, h: (ctx_blk0 + b, 2 * nh + h))],
        out_specs=pl.BlockSpec((cfg.ctx_len, hw), lambda b, h: (b, h)),
        out_shape=jax.ShapeDtypeStruct((cfg.t_ctx, d), BF16),
        compiler_params=_params("parallel", "parallel"),
        name="diff_context",
    )(lam_vecs, subln_g, qkv, qkv, qkv)
    return o_lat, o_ctx


def _conv_kernel(gb_ref, gc_ref, u_ref, gcp_ref, up_ref, gcn_ref, un_ref, w_ref, o_ref,
                 *, blocks_per_seq, lat_blocks):
    i = pl.program_id(1)
    rb = gc_ref.shape[0]
    v = gc_ref[...].astype(F32) * u_ref[...].astype(F32)
    pos = jnp.where(i < lat_blocks, i % blocks_per_seq, 0)
    last = jnp.where(i < lat_blocks, blocks_per_seq - 1, 0)
    prev_row = jnp.where(pos > 0, gcp_ref[7:8, :].astype(F32) * up_ref[7:8, :].astype(F32), 0.0)
    next_row = jnp.where(pos < last, gcn_ref[0:1, :].astype(F32) * un_ref[0:1, :].astype(F32), 0.0)
    row = lax.broadcasted_iota(I32, v.shape, 0)
    v_prev = jnp.where(row == 0, prev_row, pltpu.roll(v, 1, 0))
    v_next = jnp.where(row == rb - 1, next_row, pltpu.roll(v, rb - 1, 0))
    w = w_ref[...]
    z = w[0:1] * v_prev + w[1:2] * v + w[2:3] * v_next
    o_ref[...] = (gb_ref[...].astype(F32) * z).astype(o_ref.dtype)


def _conv_gate(cfg, proj, conv_w, layer_j):
    t = proj.shape[0]
    d = cfg.d
    rb = _tile(cfg.ctx_len, 256)
    cb = _tile(d, 1024)
    ncb = d // cb
    sub = rb // 8
    last8 = t // 8 - 1
    blocks_per_seq = cfg.seq // rb
    lat_blocks = cfg.t_lat // rb
    assert cfg.ctx_len == rb, "context sequences must be exactly one row block"

    def main(part):
        return pl.BlockSpec((rb, cb), lambda j, i: (i, part * ncb + j))

    def halo(part, nxt):
        if nxt:
            return pl.BlockSpec((8, cb), lambda j, i: (jnp.minimum((i + 1) * sub, last8), part * ncb + j))
        return pl.BlockSpec((8, cb), lambda j, i: (jnp.maximum(i * sub - 1, 0), part * ncb + j))

    return pl.pallas_call(
        functools.partial(_conv_kernel, blocks_per_seq=blocks_per_seq, lat_blocks=lat_blocks),
        grid=(ncb, t // rb),
        in_specs=[main(0), main(1), main(2), halo(1, False), halo(2, False),
                  halo(1, True), halo(2, True),
                  pl.BlockSpec((None, CONV_W, cb), lambda j, i: (layer_j, 0, j))],
        out_specs=pl.BlockSpec((rb, cb), lambda j, i: (i, j)),
        out_shape=jax.ShapeDtypeStruct((t, d), BF16),
        compiler_params=_params("parallel", "parallel"),
        name="conv_gate",
    )(proj, proj, proj, proj, proj, proj, proj, conv_w)


def _route_kernel(logit_ref, bias_ref, eidx_ref, wts_ref, rank_ref, cnt_ref, run_ref):
    @pl.when(pl.program_id(0) == 0)
    def _():
        run_ref[...] = jnp.zeros_like(run_ref)

    ne, cols = logit_ref.shape
    pg = ne // N_GROUPS
    shape3 = (N_GROUPS, pg, cols)
    scores = jax.nn.sigmoid(logit_ref[...])
    s3 = scores.reshape(shape3)
    b3 = (scores + bias_ref[...]).reshape(shape3)
    sub = lax.broadcasted_iota(I32, shape3, 1).astype(F32)
    eid3 = (lax.broadcasted_iota(I32, shape3, 0) * pg).astype(F32) + sub
    neg = jnp.float32(-jnp.inf)

    def over_experts(fn, v3):
        return fn(fn(v3, axis=0), axis=0, keepdims=True)

    m1 = jnp.max(b3, axis=1, keepdims=True)
    i1 = jnp.min(jnp.where(b3 == m1, sub, float(pg)), axis=1, keepdims=True)
    m2 = jnp.max(jnp.where(sub == i1, neg, b3), axis=1, keepdims=True)
    gs = (m1 + m2).reshape(N_GROUPS, cols)
    gi = lax.broadcasted_iota(I32, (N_GROUPS, cols), 0)
    beaten = jnp.zeros((N_GROUPS, cols), I32)
    for h in range(N_GROUPS):
        gh = gs[h:h + 1, :]
        wins = (gh > gs) | ((gh == gs) & (gi > h))
        beaten = beaten + wins.astype(I32)
    keep = (beaten < TOPK_GROUPS).reshape(N_GROUPS, 1, cols)
    cand = jnp.where(keep, b3, MASK_VALUE)
    hits, ids, ws = [], [], []
    for k in range(TOP_K):
        m = over_experts(jnp.max, cand)
        idx = over_experts(jnp.min, jnp.where(cand == m, eid3, float(ne)))
        hit = eid3 == idx
        ws.append(over_experts(jnp.sum, jnp.where(hit, s3, 0.0)))
        ids.append(idx)
        hits.append(hit)
        cand = jnp.where(hit, neg, cand)
    wsel = jnp.concatenate(ws, axis=0)
    total = jnp.sum(wsel, axis=0, keepdims=True)
    eidx_ref[...] = jnp.concatenate(ids, axis=0).astype(I32)
    wts_ref[...] = wsel / total * ROUTED_SCALE
    chosen = hits[0]
    for hit in hits[1:]:
        chosen = chosen | hit
    chosen_f = jnp.where(chosen, 1.0, 0.0).reshape(ne, cols)
    earlier = (lax.broadcasted_iota(I32, (cols, cols), 0)
               < lax.broadcasted_iota(I32, (cols, cols), 1))
    local = jnp.dot(chosen_f.astype(BF16), jnp.where(earlier, 1.0, 0.0).astype(BF16),
                    preferred_element_type=F32)
    rank3 = (local + run_ref[...]).reshape(shape3)
    ranks = [over_experts(jnp.sum, jnp.where(hit, rank3, 0.0)) for hit in hits]
    rank_ref[...] = jnp.concatenate(ranks, axis=0).astype(I32)
    run_ref[...] = run_ref[...] + jnp.sum(chosen_f, axis=1, keepdims=True)
    cnt_ref[...] = run_ref[...]


def _route(logits_t, router_b, layer):
    ne, t = logits_t.shape
    cb = _tile(t, 256)
    kt_spec = pl.BlockSpec((TOP_K, cb), lambda i: (0, i))
    return pl.pallas_call(
        _route_kernel,
        grid=(t // cb,),
        in_specs=[pl.BlockSpec((ne, cb), lambda i: (0, i)),
                  pl.BlockSpec((None, ne, 1), lambda i: (layer, 0, 0))],
        out_specs=[kt_spec, kt_spec, kt_spec, pl.BlockSpec((ne, 1), lambda i: (0, 0))],
        out_shape=[jax.ShapeDtypeStruct((TOP_K, t), I32),
                   jax.ShapeDtypeStruct((TOP_K, t), F32),
                   jax.ShapeDtypeStruct((TOP_K, t), I32),
                   jax.ShapeDtypeStruct((ne, 1), F32)],
        scratch_shapes=[pltpu.VMEM((ne, 1), F32)],
        compiler_params=_params("arbitrary"),
        name="route",
    )(logits_t, router_b)


def _dispatch_plan(eidx, rank, counts_f, tile):
    ne = counts_f.shape[0]
    n_rows = eidx.shape[1] * TOP_K
    n_tiles = n_rows // tile
    n_work = n_tiles + ne
    counts = counts_f.reshape(ne).astype(I32)
    ends = jnp.cumsum(counts)
    offs = ends - counts
    e_ids = jnp.arange(ne, dtype=I32)
    pos = (rank + jnp.sum(jnp.where(eidx[..., None] == e_ids, offs, 0), axis=-1)).T
    first_tile = offs // tile
    last_tile = jnp.maximum(ends - 1, offs) // tile
    n_items = jnp.where(counts > 0, last_tile - first_tile + 1, 0)
    item_end = jnp.cumsum(n_items)
    item_off = item_end - n_items
    total = item_end[-1]
    w = jnp.arange(n_work, dtype=I32)
    e_of = jnp.minimum(jnp.sum((item_end[None, :] <= w[:, None]).astype(I32), axis=1), ne - 1)
    last_e = jnp.max(jnp.where(counts > 0, e_ids, 0)).astype(I32)
    live = w < total
    e_of = jnp.where(live, e_of, last_e)
    sel = e_of[:, None] == e_ids

    def at_e(table):
        return jnp.sum(jnp.where(sel, table, 0), axis=1)

    tile_of = jnp.where(live, at_e(first_tile) + (w - at_e(item_off)), n_tiles - 1).astype(I32)
    lo = jnp.clip(at_e(offs) - tile_of * tile, 0, tile)
    hi = jnp.clip(at_e(ends) - tile_of * tile, 0, tile)
    lo = jnp.where(live, lo, 0).astype(I32)
    hi = jnp.where(live, hi, 0).astype(I32)
    prev_tile = jnp.concatenate([jnp.full((1,), -1, I32), tile_of[:-1]])
    prev_e = jnp.concatenate([jnp.full((1,), -1, I32), e_of[:-1]])
    new_tile = (tile_of != prev_tile).astype(I32)
    new_e = (e_of != prev_e).astype(I32)
    return pos, (tile_of, e_of, lo, hi, new_tile, new_e)


def _dispatch_kernel(pos_ref, f_ref, xs_ref, sem):
    groups = f_ref.shape[0]

    def issue(g, carry):
        for s in range(SUBLANES):
            for k in range(TOP_K):
                p = pos_ref[0, (g * SUBLANES + s) * TOP_K + k]
                pltpu.make_async_copy(f_ref.at[g, pl.ds(s, 1)], xs_ref.at[pl.ds(p, 1)],
                                      sem).start(priority=k % 2)
        return carry

    lax.fori_loop(0, groups, issue, 0)
    rows = groups * SUBLANES
    for k in range(TOP_K):
        pltpu.make_async_copy(xs_ref.at[pl.ds(0, rows)], xs_ref.at[pl.ds(0, rows)], sem).wait()


def _dispatch(fp, pos):
    t, dw = fp.shape
    rb = _tile(t, 256)
    nb = t // rb
    return pl.pallas_call(
        _dispatch_kernel,
        grid=(nb,),
        in_specs=[pl.BlockSpec((None, 1, rb * TOP_K), lambda i: (i, 0, 0), memory_space=pltpu.SMEM),
                  pl.BlockSpec((rb // SUBLANES, SUBLANES, dw), lambda i: (i, 0, 0))],
        out_specs=pl.BlockSpec(memory_space=pl.ANY),
        out_shape=jax.ShapeDtypeStruct((t * TOP_K, dw), U32),
        scratch_shapes=[pltpu.SemaphoreType.DMA(())],
        compiler_params=_params("arbitrary"),
        name="moe_dispatch",
    )(pos.reshape(nb, 1, rb * TOP_K), fp.reshape(t // SUBLANES, SUBLANES, dw))


def _row_mask(lo_ref, hi_ref, w, rows):
    r = lax.broadcasted_iota(I32, (rows, 1), 0)
    return (r >= lo_ref[w]) & (r < hi_ref[w])


def _gmm_up_kernel(tile_ref, e_ref, lo_ref, hi_ref, nt_ref, ne_ref,
                   x_ref, wgt_ref, wut_ref, o_ref, wcat_ref):
    w = pl.program_id(0)
    ff = wgt_ref.shape[0]

    @pl.when(ne_ref[w] == 1)
    def _():
        wcat_ref[0:ff] = wgt_ref[...].astype(BF16)
        wcat_ref[ff:2 * ff] = wut_ref[...].astype(BF16)

    lo, hi = _unpack_bf16_pair(x_ref[...])
    lo, hi = lo.astype(BF16), hi.astype(BF16)
    rows, half = lo.shape
    ht = (lax.dot_general(wcat_ref[:, :half], lo, _NT, preferred_element_type=F32)
          + lax.dot_general(wcat_ref[:, half:], hi, _NT, preferred_element_type=F32))
    act_t = _silu(ht[:ff]) * ht[ff:]
    pad = (-ff) % LANE
    if pad:
        act_t = jnp.concatenate([act_t, jnp.zeros((pad, rows), F32)], axis=0)
    act = act_t.T[:, :ff].astype(o_ref.dtype)
    m = _row_mask(lo_ref, hi_ref, w, act.shape[0])

    @pl.when(nt_ref[w] == 1)
    def _():
        o_ref[...] = jnp.where(m, act, jnp.zeros_like(act))

    @pl.when(nt_ref[w] == 0)
    def _():
        o_ref[...] = jnp.where(m, act, o_ref[...])


def _gmm_down_kernel(tile_ref, e_ref, lo_ref, hi_ref, nt_ref, ne_ref,
                     h_ref, wd_ref, o_ref, wdb_ref):
    w = pl.program_id(0)

    @pl.when(ne_ref[w] == 1)
    def _():
        wdb_ref[...] = wd_ref[...].astype(BF16)

    y = jnp.dot(h_ref[...], wdb_ref[...], preferred_element_type=F32)
    half = y.shape[1] // 2
    yp = _pack_bf16_pair(y[:, :half], y[:, half:])
    m = _row_mask(lo_ref, hi_ref, w, yp.shape[0])

    @pl.when(nt_ref[w] == 1)
    def _():
        o_ref[...] = jnp.where(m, yp, jnp.zeros_like(yp))

    @pl.when(nt_ref[w] == 0)
    def _():
        o_ref[...] = jnp.where(m, yp, o_ref[...])


def _gmm_up(xs_sorted, plan, exp_gate_t, exp_up_t, layer):
    n_rows, dw = xs_sorted.shape
    ff, d = exp_gate_t.shape[2], exp_gate_t.shape[3]
    n_work = plan[0].shape[0]
    w_spec = pl.BlockSpec((None, None, ff, d), lambda w, tl, e, *_: (layer, e[w], 0, 0))
    return pl.pallas_call(
        _gmm_up_kernel,
        grid_spec=pltpu.PrefetchScalarGridSpec(
            num_scalar_prefetch=6,
            grid=(n_work,),
            in_specs=[pl.BlockSpec((GMM_TILE, dw), lambda w, tl, *_: (tl[w], 0)), w_spec, w_spec],
            out_specs=pl.BlockSpec((GMM_TILE, ff), lambda w, tl, *_: (tl[w], 0)),
            scratch_shapes=[pltpu.VMEM((2 * ff, d), BF16)],
        ),
        out_shape=jax.ShapeDtypeStruct((n_rows, ff), BF16),
        compiler_params=_params("arbitrary"),
        name="moe_up",
    )(*plan, xs_sorted, exp_gate_t, exp_up_t)


def _gmm_down(h_sorted, plan, exp_down, layer):
    n_rows, ff = h_sorted.shape
    d = exp_down.shape[3]
    n_work = plan[0].shape[0]
    return pl.pallas_call(
        _gmm_down_kernel,
        grid_spec=pltpu.PrefetchScalarGridSpec(
            num_scalar_prefetch=6,
            grid=(n_work,),
            in_specs=[pl.BlockSpec((GMM_TILE, ff), lambda w, tl, *_: (tl[w], 0)),
                      pl.BlockSpec((None, None, ff, d), lambda w, tl, e, *_: (layer, e[w], 0, 0))],
            out_specs=pl.BlockSpec((GMM_TILE, d // 2), lambda w, tl, *_: (tl[w], 0)),
            scratch_shapes=[pltpu.VMEM((ff, d), BF16)],
        ),
        out_shape=jax.ShapeDtypeStruct((n_rows, d // 2), U32),
        compiler_params=_params("arbitrary"),
        name="moe_down",
    )(*plan, h_sorted, exp_down)


def _combine_gather(pos_ref, ys_ref, buf_ref, sem, slot):
    groups = buf_ref.shape[2]

    def issue(g, carry):
        for s in range(SUBLANES):
            for k in range(TOP_K):
                p = pos_ref[0, (g * SUBLANES + s) * TOP_K + k]
                pltpu.make_async_copy(ys_ref.at[pl.ds(p, 1)], buf_ref.at[slot, k, g, pl.ds(s, 1)],
                                      sem.at[slot]).start(priority=k % 2)
        return carry

    lax.fori_loop(0, groups, issue, 0)


def _combine_moe(pos_ref, posn_ref, wts_ref, ys_ref, x_ref, hsh_ref, sd_ref, g2_ref,
                 xo_ref, buf_ref, sem):
    i = pl.program_id(0)
    groups, width = buf_ref.shape[2], buf_ref.shape[4]
    rows = groups * SUBLANES
    slot = i % 2

    @pl.when(i == 0)
    def _():
        _combine_gather(pos_ref, ys_ref, buf_ref, sem, 0)

    @pl.when(i + 1 < pl.num_programs(0))
    def _():
        _combine_gather(posn_ref, ys_ref, buf_ref, sem, 1 - slot)

    wts = wts_ref[...]
    wcols = [jnp.broadcast_to(wts[:, k:k + 1], (rows, LANE)) for k in range(TOP_K)]
    hsh = hsh_ref[...]
    for k in range(TOP_K):
        pltpu.make_async_copy(ys_ref.at[pl.ds(0, rows)], ys_ref.at[pl.ds(0, rows)],
                              sem.at[slot]).wait()
    ssq = jnp.zeros((rows, 1), F32)
    for c in range(width // LANE):
        lo_sl = slice(c * LANE, (c + 1) * LANE)
        hi_sl = slice(width + c * LANE, width + (c + 1) * LANE)
        acc_lo = jnp.dot(hsh, sd_ref[:, lo_sl], preferred_element_type=F32)
        acc_hi = jnp.dot(hsh, sd_ref[:, hi_sl], preferred_element_type=F32)
        for k in range(TOP_K):
            u = buf_ref[slot, k, :, :, lo_sl].reshape(rows, LANE)
            lo, hi = _unpack_bf16_pair(u)
            acc_lo = acc_lo + wcols[k] * lo
            acc_hi = acc_hi + wcols[k] * hi
        for sl, acc in ((lo_sl, acc_lo), (hi_sl, acc_hi)):
            xn = x_ref[:, sl] + g2_ref[:, sl] * acc
            xo_ref[:, sl] = xn
            ssq = ssq + jnp.sum(xn * xn, axis=-1, keepdims=True)
    return ssq


def _combine_kernel(pos_ref, posn_ref, wts_ref, ys_ref, x_ref, hsh_ref, sd_ref, g2_ref,
                    ng_ref, sc_ref, sh_ref, xo_ref, a_ref, buf_ref, sem):
    ssq = _combine_moe(pos_ref, posn_ref, wts_ref, ys_ref, x_ref, hsh_ref, sd_ref, g2_ref,
                       xo_ref, buf_ref, sem)
    inv = lax.rsqrt(ssq * (1.0 / x_ref.shape[1]) + NORM_EPS)
    y = (xo_ref[...] * inv) * ng_ref[...]
    a_ref[...] = (y * (1.0 + sc_ref[...]) + sh_ref[...]).astype(a_ref.dtype)


def _combine_final_kernel(pos_ref, posn_ref, wts_ref, ys_ref, x_ref, hsh_ref, sd_ref, g2_ref,
                          ng_ref, o_ref, xn_ref, buf_ref, sem):
    ssq = _combine_moe(pos_ref, posn_ref, wts_ref, ys_ref, x_ref, hsh_ref, sd_ref, g2_ref,
                       xn_ref, buf_ref, sem)
    inv = lax.rsqrt(ssq * (1.0 / x_ref.shape[1]) + NORM_EPS)
    o_ref[...] = (xn_ref[...] * inv) * ng_ref[...]


def _combine(cfg, xs, ys_sorted, pos, wts, hsh, sh_down_bf, mod, layer, next_g, final):
    t, d = xs.shape
    ff = hsh.shape[1]
    rb = _tile(cfg.ctx_len, 128)
    nb = t // rb
    pos3 = pos.reshape(nb, 1, rb * TOP_K)
    row_spec = pl.BlockSpec((rb, d), lambda i: (i, 0))
    in_specs = [pl.BlockSpec((None, 1, rb * TOP_K), lambda i: (i, 0, 0), memory_space=pltpu.SMEM),
                pl.BlockSpec((None, 1, rb * TOP_K), lambda i: (jnp.minimum(i + 1, nb - 1), 0, 0),
                             memory_space=pltpu.SMEM),
                pl.BlockSpec((rb, TOP_K), lambda i: (i, 0)),
                pl.BlockSpec(memory_space=pl.ANY),
                row_spec,
                pl.BlockSpec((rb, ff), lambda i: (i, 0)),
                pl.BlockSpec((None, ff, d), lambda i: (layer, 0, 0)),
                _mod_spec(cfg, layer, 5, rb, d)]
    scratch = [pltpu.VMEM((2, TOP_K, rb // SUBLANES, SUBLANES, d // 2), U32),
               pltpu.SemaphoreType.DMA((2,))]
    args = (pos3, pos3, wts, ys_sorted, xs, hsh, sh_down_bf, mod)
    if final:
        return pl.pallas_call(
            _combine_final_kernel,
            grid=(nb,),
            in_specs=in_specs + [pl.BlockSpec((1, d), lambda i: (0, 0))],
            out_specs=row_spec,
            out_shape=jax.ShapeDtypeStruct((t, d), F32),
            scratch_shapes=[pltpu.VMEM((rb, d), F32)] + scratch,
            compiler_params=_params("arbitrary"),
            name="moe_combine_final",
        )(*args, next_g)
    nxt = layer + 1
    return pl.pallas_call(
        _combine_kernel,
        grid=(nb,),
        in_specs=in_specs + [pl.BlockSpec((None, 1, d), lambda i: (nxt, 0, 0)),
                             _mod_spec(cfg, nxt, 1, rb, d), _mod_spec(cfg, nxt, 0, rb, d)],
        out_specs=[row_spec, row_spec],
        out_shape=[jax.ShapeDtypeStruct((t, d), F32), jax.ShapeDtypeStruct((t, d), BF16)],
        scratch_shapes=scratch,
        compiler_params=_params("arbitrary"),
        name="moe_combine",
    )(*args, next_g, mod, mod)


def _rope_tables(seq):
    rows = seq // GRID_W
    row = jnp.broadcast_to(jnp.arange(rows)[:, None], (rows, GRID_W)).reshape(-1).astype(F32)
    col = jnp.broadcast_to(jnp.arange(GRID_W)[None, :], (rows, GRID_W)).reshape(-1).astype(F32)
    n_freq = HEAD_DIM // 4
    inv = ROPE_THETA ** (-jnp.arange(n_freq, dtype=F32) / n_freq)
    ang = jnp.concatenate([row[:, None] * inv, col[:, None] * inv], axis=-1)
    cos, sin = jnp.cos(ang), jnp.sin(ang)
    return jnp.concatenate([cos, cos], axis=-1), jnp.concatenate([-sin, sin], axis=-1)


def kernel(x, c, ctx, c_ctx, ada_down, ada_up, ada_b, norm1_g, norm2_g, a_w_qkv, a_w_o, a_sink,
           b_w_qkv, b_w_o, b_lam, b_subln_g, c_w_in, c_conv, c_w_out, router_w, router_b,
           exp_gate, exp_up, exp_down, sh_gate, sh_up, sh_down, final_g):
    batch, seq, d = x.shape
    ctx_len = ctx.shape[1]
    depth = ada_down.shape[0]
    cfg = _Cfg(batch, seq, ctx_len, d)
    assert batch + 1 <= MOD_ROWS

    xs = jnp.concatenate([x.reshape(-1, d), ctx.reshape(-1, d)], axis=0)
    src = jnp.concatenate([c, c_ctx[None, :], jnp.zeros((MOD_ROWS - batch - 1, d), F32)], axis=0)
    mod = _ada_table(src, ada_down, ada_up, ada_b).reshape(depth * MOD_ROWS, 1, 6 * d)
    cos_t, sin_t = _rope_tables(seq)
    norm1_g3 = norm1_g.reshape(depth, 1, d)
    norm2_g3 = norm2_g.reshape(depth, 1, d)
    router_b3 = router_b.reshape(depth, -1, 1)
    b_subln_g3 = b_subln_g.reshape(b_subln_g.shape[0], 1, -1)
    sh_gate_bf, sh_up_bf, sh_down_bf = (w.astype(BF16) for w in (sh_gate, sh_up, sh_down))
    router_hi = router_w.astype(BF16)
    router_lo = (router_w - router_hi.astype(F32)).astype(BF16)
    router_hl = jnp.concatenate([router_hi, router_lo], axis=-1)
    exp_gate_t, exp_up_t = jnp.swapaxes(exp_gate, 2, 3), jnp.swapaxes(exp_up, 2, 3)

    a = _norm1(cfg, xs, norm1_g3, mod, 0)
    out = None
    for i in range(depth):
        kind, j = i % N_MIXERS, i // N_MIXERS
        need_ctx = i < depth - 1
        if kind == 0:
            nq = d
            nk = d // KV_GROUP
            qkv = _proj(cfg, a, a_w_qkv, j, rope=(cos_t, sin_t, nq + nk))
            o_lat, o_ctx = _gqa_attention(cfg, qkv, a_sink, j, need_ctx)
            w_o = a_w_o
        elif kind == 1:
            lam_init = 0.8 - 0.6 * math.exp(-0.3 * i)
            qkv = _proj(cfg, a, b_w_qkv, j, rope=(cos_t, sin_t, 2 * d))
            o_lat, o_ctx = _diff_attention(cfg, qkv, b_lam, b_subln_g3, j, lam_init, need_ctx)
            w_o = b_w_o
        else:
            pr = _proj(cfg, a, c_w_in, j)
            o = _conv_gate(cfg, pr, c_conv, j)
            w_o = c_w_out
        if kind != 2:
            o = o_lat if o_ctx is None else jnp.concatenate([o_lat, o_ctx], axis=0)
        elif not need_ctx:
            o = o[:cfg.t_lat]
        xs = _proj_resid(cfg, o, w_o, j, xs, mod, i, 2)

        fp, logits, hsh = _norm2(cfg, xs, norm2_g3, mod, i, router_hl, sh_gate_bf, sh_up_bf)
        eidx_t, wts_t, rank_t, counts = _route(logits, router_b3, i)
        pos, plan = _dispatch_plan(eidx_t, rank_t, counts, GMM_TILE)
        wts = wts_t.T
        xs_sorted = _dispatch(fp, pos)
        h_sorted = _gmm_up(xs_sorted, plan, exp_gate_t, exp_up_t, i)
        ys_sorted = _gmm_down(h_sorted, plan, exp_down, i)
        if i == depth - 1:
            out = _combine(cfg, xs, ys_sorted, pos, wts, hsh, sh_down_bf, mod, i,
                           final_g.reshape(1, d), True)
        else:
            xs, a = _combine(cfg, xs, ys_sorted, pos, wts, hsh, sh_down_bf, mod, i, norm1_g3, False)
    return out.reshape(batch, seq, d)
```

```python
import functools
import math

import jax
import jax.numpy as jnp
from jax import lax
from jax.experimental import pallas as pl
from jax.experimental.pallas import tpu as pltpu

F32 = jnp.float32
BF16 = jnp.bfloat16
U32 = jnp.uint32
I32 = jnp.int32

HEAD_DIM = 128
GRID_W = 64
ROPE_THETA = 10000.0
NORM_EPS = 1e-6
MASK_VALUE = -1e30
WINDOW = 128
BLOCK = 128
CONV_W = 3
N_MIXERS = 3
N_EXPERTS = 64
TOP_K = 8
N_GROUPS = 8
TOPK_GROUPS = 4
ROUTED_SCALE = 2.5
KV_GROUP = 4
MOD_ROWS = 8

LANE = 128
SUBLANES = 8
VMEM_LIMIT_BYTES = 56 * 1024 * 1024
GMM_TILE = 256
HI_MASK = 0xFFFF0000
LOG2E = 1.4426950408889634
QUERY_SCALE = HEAD_DIM ** -0.5 * LOG2E


def _params(*sem):
    return pltpu.CompilerParams(dimension_semantics=sem, vmem_limit_bytes=VMEM_LIMIT_BYTES)


def _tile(n, pref):
    t = min(n, pref)
    while n % t:
        t //= 2
    return t


def _silu(v):
    return v * jax.nn.sigmoid(v)


def _pack_bf16_pair(lo, hi):
    lo_b = lax.bitcast_convert_type(lo.astype(BF16).astype(F32), U32)
    hi_b = lax.bitcast_convert_type(hi.astype(BF16).astype(F32), U32)
    return (lo_b >> 16) | (hi_b & jnp.uint32(HI_MASK))


def _unpack_bf16_pair(u):
    lo = lax.bitcast_convert_type(u << 16, F32)
    hi = lax.bitcast_convert_type(u & jnp.uint32(HI_MASK), F32)
    return lo, hi


def _ada_kernel(src_ref, down_ref, up_ref, b_ref, out_ref, t1_ref):
    @pl.when(pl.program_id(1) == 0)
    def _():
        s = _silu(src_ref[...]).astype(BF16)
        t1 = jnp.dot(s, down_ref[...].astype(BF16), preferred_element_type=F32)
        t1_ref[...] = t1.astype(BF16)

    out_ref[...] = jnp.dot(t1_ref[...], up_ref[...].astype(BF16),
                           preferred_element_type=F32) + b_ref[...]


def _ada_table(src, ada_down, ada_up, ada_b):
    depth, d, rank = ada_down.shape
    n6 = ada_up.shape[2]
    tn = _tile(n6, 2048)
    return pl.pallas_call(
        _ada_kernel,
        grid=(depth, n6 // tn),
        in_specs=[
            pl.BlockSpec((MOD_ROWS, d), lambda l, j: (0, 0)),
            pl.BlockSpec((None, d, rank), lambda l, j: (l, 0, 0)),
            pl.BlockSpec((None, rank, tn), lambda l, j: (l, 0, j)),
            pl.BlockSpec((None, 1, tn), lambda l, j: (l, 0, j)),
        ],
        out_specs=pl.BlockSpec((None, MOD_ROWS, tn), lambda l, j: (l, 0, j)),
        out_shape=jax.ShapeDtypeStruct((depth, MOD_ROWS, n6), F32),
        scratch_shapes=[pltpu.VMEM((MOD_ROWS, rank), BF16)],
        compiler_params=_params("arbitrary", "arbitrary"),
        name="ada_table",
    )(src, ada_down, ada_up, ada_b.reshape(depth, 1, n6))


def _norm_mod(xf, g, sc, sh):
    y = xf * lax.rsqrt(jnp.mean(xf * xf, axis=-1, keepdims=True) + NORM_EPS)
    return (y * g) * (1.0 + sc) + sh


class _Cfg:
    def __init__(self, batch, seq, ctx_len, d):
        self.batch, self.seq, self.ctx_len, self.d = batch, seq, ctx_len, d
        self.t_lat = batch * seq
        self.t_ctx = batch * ctx_len
        self.t = self.t_lat + self.t_ctx

    def group_of_rows(self, rows_per_block):
        per_seq = self.seq // rows_per_block
        assert per_seq * rows_per_block == self.seq
        assert self.t_lat % rows_per_block == 0 and self.t_ctx % rows_per_block == 0
        batch = self.batch
        return lambda i: jnp.minimum(i // per_seq, batch)


def _mod_spec(cfg, layer, chunk, rows_per_block, width, col_of=None):
    grp = cfg.group_of_rows(rows_per_block)
    per_chunk = cfg.d // width
    if col_of is None:
        return pl.BlockSpec((None, 1, width),
                            lambda i: (layer * MOD_ROWS + grp(i), 0, chunk * per_chunk))
    return pl.BlockSpec((None, 1, width),
                        lambda j, i: (layer * MOD_ROWS + grp(i), 0, chunk * per_chunk + j))


def _norm1_kernel(x_ref, g_ref, sc_ref, sh_ref, a_ref):
    a_ref[...] = _norm_mod(x_ref[...], g_ref[...], sc_ref[...], sh_ref[...]).astype(BF16)


def _norm1(cfg, xs, norm_g, mod, layer):
    t, d = xs.shape
    rb = _tile(cfg.ctx_len, 256)
    return pl.pallas_call(
        _norm1_kernel,
        grid=(t // rb,),
        in_specs=[
            pl.BlockSpec((rb, d), lambda i: (i, 0)),
            pl.BlockSpec((None, 1, d), lambda i: (layer, 0, 0)),
            _mod_spec(cfg, layer, 1, rb, d),
            _mod_spec(cfg, layer, 0, rb, d),
        ],
        out_specs=pl.BlockSpec((rb, d), lambda i: (i, 0)),
        out_shape=jax.ShapeDtypeStruct((t, d), BF16),
        compiler_params=_params("parallel"),
        name="norm1",
    )(xs, norm_g, mod, mod)


def _norm2_kernel(x_ref, g_ref, sc_ref, sh_ref, rw_ref, sg_ref, su_ref,
                  fp_ref, logit_ref, hsh_ref):
    f = _norm_mod(x_ref[...], g_ref[...], sc_ref[...], sh_ref[...])
    half = f.shape[1] // 2
    fp_ref[...] = _pack_bf16_pair(f[:, :half], f[:, half:])
    fb = f.astype(BF16)
    f_lo = (f - fb.astype(F32)).astype(BF16)
    ne = logit_ref.shape[0]
    both = (jnp.dot(fb, rw_ref[...], preferred_element_type=F32)
            + jnp.dot(f_lo, rw_ref[...], preferred_element_type=F32))
    both_t = both.T
    logit_ref[...] = both_t[:ne] + both_t[ne:]
    hg = jnp.dot(fb, sg_ref[...], preferred_element_type=F32)
    hu = jnp.dot(fb, su_ref[...], preferred_element_type=F32)
    hsh_ref[...] = (_silu(hg) * hu).astype(BF16)


def _norm2(cfg, xs, norm_g, mod, layer, router_w, sh_gate_bf, sh_up_bf):
    t, d = xs.shape
    ne = router_w.shape[2] // 2
    ff = sh_gate_bf.shape[2]
    rb = _tile(cfg.ctx_len, 256)
    return pl.pallas_call(
        _norm2_kernel,
        grid=(t // rb,),
        in_specs=[
            pl.BlockSpec((rb, d), lambda i: (i, 0)),
            pl.BlockSpec((None, 1, d), lambda i: (layer, 0, 0)),
            _mod_spec(cfg, layer, 4, rb, d),
            _mod_spec(cfg, layer, 3, rb, d),
            pl.BlockSpec((None, d, 2 * ne), lambda i: (layer, 0, 0)),
            pl.BlockSpec((None, d, ff), lambda i: (layer, 0, 0)),
            pl.BlockSpec((None, d, ff), lambda i: (layer, 0, 0)),
        ],
        out_specs=[
            pl.BlockSpec((rb, d // 2), lambda i: (i, 0)),
            pl.BlockSpec((ne, rb), lambda i: (0, i)),
            pl.BlockSpec((rb, ff), lambda i: (i, 0)),
        ],
        out_shape=[
            jax.ShapeDtypeStruct((t, d // 2), U32),
            jax.ShapeDtypeStruct((ne, t), F32),
            jax.ShapeDtypeStruct((t, ff), BF16),
        ],
        compiler_params=_params("parallel"),
        name="norm2_router",
    )(xs, norm_g, mod, mod, router_w, sh_gate_bf, sh_up_bf)


def _cast_weight(w_ref, wb_ref):
    @pl.when(pl.program_id(1) == 0)
    def _():
        wb_ref[...] = w_ref[...].astype(BF16)


def _proj_kernel(x_ref, w_ref, o_ref, wb_ref):
    _cast_weight(w_ref, wb_ref)
    o_ref[...] = jnp.dot(x_ref[...], wb_ref[...], preferred_element_type=F32).astype(o_ref.dtype)


def _proj_rope_kernel(x_ref, w_ref, cos_ref, sin_ref, o_ref, wb_ref, *, rope_cols, rope_rows, q_cols):
    _cast_weight(w_ref, wb_ref)
    acc = jnp.dot(x_ref[...], wb_ref[...], preferred_element_type=F32)
    acc = acc * jnp.where(pl.program_id(0) < q_cols, QUERY_SCALE, 1.0)
    do_rope = jnp.logical_and(pl.program_id(0) < rope_cols, pl.program_id(1) < rope_rows)

    @pl.when(do_rope)
    def _():
        c, s = cos_ref[...], sin_ref[...]
        for h in range(acc.shape[1] // HEAD_DIM):
            sl = slice(h * HEAD_DIM, (h + 1) * HEAD_DIM)
            xh = acc[:, sl]
            o_ref[:, sl] = (xh * c + pltpu.roll(xh, HEAD_DIM // 2, 1) * s).astype(o_ref.dtype)

    @pl.when(jnp.logical_not(do_rope))
    def _():
        o_ref[...] = acc.astype(o_ref.dtype)


def _proj_resid_kernel(x_ref, w_ref, res_ref, gate_ref, o_ref, wb_ref):
    _cast_weight(w_ref, wb_ref)
    acc = jnp.dot(x_ref[...], wb_ref[...], preferred_element_type=F32)
    o_ref[...] = res_ref[...] + gate_ref[...] * acc


def _proj_tiles(m, n, seq, n_align=0):
    return _tile(math.gcd(m, seq), 1024), _tile(math.gcd(n, n_align), 512)


def _proj(cfg, x, w, layer_j, rope=None):
    m, k = x.shape
    n = w.shape[2]
    tm, tn = _proj_tiles(m, n, cfg.seq, 0 if rope is None else rope[2])
    grid = (n // tn, m // tm)
    x_spec = pl.BlockSpec((tm, k), lambda j, i: (i, 0))
    w_spec = pl.BlockSpec((None, k, tn), lambda j, i: (layer_j, 0, j))
    o_spec = pl.BlockSpec((tm, tn), lambda j, i: (i, j))
    common = dict(
        grid=grid, out_specs=o_spec,
        out_shape=jax.ShapeDtypeStruct((m, n), BF16),
        scratch_shapes=[pltpu.VMEM((k, tn), BF16)],
        compiler_params=_params("arbitrary", "arbitrary"),
    )
    if rope is None:
        return pl.pallas_call(_proj_kernel, in_specs=[x_spec, w_spec], name="proj", **common)(x, w)
    cos_t, sin_t, rope_n = rope
    per_seq = cfg.seq // tm
    t_spec = pl.BlockSpec((tm, HEAD_DIM), lambda j, i: (i % per_seq, 0))
    assert cfg.d % tn == 0
    kern = functools.partial(_proj_rope_kernel, rope_cols=rope_n // tn, rope_rows=cfg.t_lat // tm,
                             q_cols=cfg.d // tn)
    return pl.pallas_call(kern, in_specs=[x_spec, w_spec, t_spec, t_spec],
                          name="proj_rope", **common)(x, w, cos_t, sin_t)


def _proj_resid(cfg, x, w, layer_j, res, mod, layer, gate_chunk):
    m, k = x.shape
    n = w.shape[2]
    tm, tn = _proj_tiles(m, n, cfg.seq)
    return pl.pallas_call(
        _proj_resid_kernel,
        grid=(n // tn, m // tm),
        in_specs=[
            pl.BlockSpec((tm, k), lambda j, i: (i, 0)),
            pl.BlockSpec((None, k, tn), lambda j, i: (layer_j, 0, j)),
            pl.BlockSpec((tm, tn), lambda j, i: (i, j)),
            _mod_spec(cfg, layer, gate_chunk, tm, tn, col_of=True),
        ],
        out_specs=pl.BlockSpec((tm, tn), lambda j, i: (i, j)),
        out_shape=jax.ShapeDtypeStruct((m, n), F32),
        scratch_shapes=[pltpu.VMEM((k, tn), BF16)],
        compiler_params=_params("arbitrary", "arbitrary"),
        name="proj_resid",
    )(x, w, res, mod)


_NT = (((1,), (1,)), ((), ()))


def _gqa_head_group(q_ref, k_parts, v_parts, valid, sink_ref, layer_j, kvh, o_ref):
    dh = HEAD_DIM
    rows = q_ref.shape[0]
    heads = [kvh * KV_GROUP + g for g in range(KV_GROUP)]
    qg = jnp.concatenate([q_ref[:, h * dh:(h + 1) * dh] for h in heads], axis=0)
    scores = []
    for idx, kp in enumerate(k_parts):
        s = lax.dot_general(qg, kp, _NT, preferred_element_type=F32)
        if idx == 0 and valid is not None:
            s = jnp.where(valid, s, MASK_VALUE)
        scores.append(s)
    row_head = lax.broadcasted_iota(I32, (KV_GROUP * rows, 1), 0) // rows
    sink_col = jnp.zeros((KV_GROUP * rows, 1), F32)
    for g, h in enumerate(heads):
        sink_col = jnp.where(row_head == g, sink_ref[layer_j, h] * LOG2E, sink_col)
    m = sink_col
    for s in scores:
        m = jnp.maximum(m, jnp.max(s, axis=-1, keepdims=True))
    es = [jnp.exp2(s - m) for s in scores]
    denom = jnp.exp2(sink_col - m)
    for e in es:
        denom = denom + jnp.sum(e, axis=-1, keepdims=True)
    o = None
    for e, vp in zip(es, v_parts):
        part = jnp.dot(e.astype(BF16), vp, preferred_element_type=F32)
        o = part if o is None else o + part
    o = o * (1.0 / denom)
    for g, h in enumerate(heads):
        o_ref[:, h * dh:(h + 1) * dh] = o[g * rows:(g + 1) * rows].astype(o_ref.dtype)


def _gqa_lat_kernel(sink_ref, q_ref, kp_ref, ko_ref, kn_ref, vp_ref, vo_ref, vn_ref,
                    kc_ref, vc_ref, o_ref, *, layer_j, n_kv, n_blk):
    n = pl.program_id(1)
    dh = HEAD_DIM
    shape = (KV_GROUP * BLOCK, 3 * BLOCK)
    qi = lax.broadcasted_iota(I32, shape, 0) % BLOCK
    c = lax.broadcasted_iota(I32, shape, 1)
    rel = (c - BLOCK) - qi
    kpos = n * BLOCK - BLOCK + c
    valid = (jnp.abs(rel) <= WINDOW) & (kpos >= 0) & (kpos < n_blk * BLOCK)
    for kvh in range(n_kv):
        sl = slice(kvh * dh, (kvh + 1) * dh)
        k_lat = jnp.concatenate([kp_ref[:, sl], ko_ref[:, sl], kn_ref[:, sl]], axis=0)
        v_lat = jnp.concatenate([vp_ref[:, sl], vo_ref[:, sl], vn_ref[:, sl]], axis=0)
        _gqa_head_group(q_ref, [k_lat, kc_ref[:, sl]], [v_lat, vc_ref[:, sl]],
                        valid, sink_ref, layer_j, kvh, o_ref)


def _gqa_ctx_kernel(sink_ref, q_ref, kc_ref, vc_ref, o_ref, *, layer_j, n_kv):
    dh = HEAD_DIM
    for kvh in range(n_kv):
        sl = slice(kvh * dh, (kvh + 1) * dh)
        _gqa_head_group(q_ref, [kc_ref[:, sl]], [vc_ref[:, sl]], None, sink_ref, layer_j, kvh, o_ref)


def _gqa_attention(cfg, qkv, sink, layer_j, need_ctx):
    t = qkv.shape[0]
    d = cfg.d
    dh = HEAD_DIM
    n_kv = d // dh // KV_GROUP
    nk = n_kv * dh
    n_blk = cfg.seq // BLOCK
    kcol, vcol = d // nk, d // nk + 1
    ctx_blk0 = cfg.t_lat // cfg.ctx_len
    smem = pl.BlockSpec(memory_space=pltpu.SMEM)

    def band(col, shift):
        return pl.BlockSpec(
            (BLOCK, nk),
            lambda b, n: (b * n_blk + jnp.clip(n + shift, 0, n_blk - 1), col))

    def ctx_spec(col):
        return pl.BlockSpec((cfg.ctx_len, nk), lambda b, n: (ctx_blk0 + b, col))

    o_lat = pl.pallas_call(
        functools.partial(_gqa_lat_kernel, layer_j=layer_j, n_kv=n_kv, n_blk=n_blk),
        grid=(cfg.batch, n_blk),
        in_specs=[smem, pl.BlockSpec((BLOCK, d), lambda b, n: (b * n_blk + n, 0)),
                  band(kcol, -1), band(kcol, 0), band(kcol, 1),
                  band(vcol, -1), band(vcol, 0), band(vcol, 1),
                  ctx_spec(kcol), ctx_spec(vcol)],
        out_specs=pl.BlockSpec((BLOCK, d), lambda b, n: (b * n_blk + n, 0)),
        out_shape=jax.ShapeDtypeStruct((cfg.t_lat, d), BF16),
        compiler_params=_params("parallel", "parallel"),
        name="gqa_latent",
    )(sink, qkv, qkv, qkv, qkv, qkv, qkv, qkv, qkv, qkv)
    if not need_ctx:
        return o_lat, None

    def ctx1(col):
        return pl.BlockSpec((cfg.ctx_len, nk), lambda b: (ctx_blk0 + b, col))

    o_ctx = pl.pallas_call(
        functools.partial(_gqa_ctx_kernel, layer_j=layer_j, n_kv=n_kv),
        grid=(cfg.batch,),
        in_specs=[smem, pl.BlockSpec((cfg.ctx_len, d), lambda b: (ctx_blk0 + b, 0)),
                  ctx1(kcol), ctx1(vcol)],
        out_specs=pl.BlockSpec((cfg.ctx_len, d), lambda b: (b, 0)),
        out_shape=jax.ShapeDtypeStruct((cfg.t_ctx, d), BF16),
        compiler_params=_params("parallel"),
        name="gqa_context",
    )(sink, qkv, qkv, qkv)
    return o_lat, o_ctx


def _diff_kernel(lam_ref, g_ref, q_ref, *refs, lam_init, n_parts):
    k_refs, v_refs, o_ref = refs[:n_parts], refs[n_parts:2 * n_parts], refs[2 * n_parts]
    dh = HEAD_DIM
    lf = lam_ref[...]
    lam = (jnp.exp(jnp.sum(lf[0:1] * lf[1:2], keepdims=True))
           - jnp.exp(jnp.sum(lf[2:3] * lf[3:4], keepdims=True)) + lam_init)
    outs = []
    for c in range(2):
        qc = q_ref[:, c * dh:(c + 1) * dh]
        ss = [lax.dot_general(qc, k_ref[:, c * dh:(c + 1) * dh], _NT,
                              preferred_element_type=F32) for k_ref in k_refs]
        m = None
        for s in ss:
            mx = jnp.max(s, axis=-1, keepdims=True)
            m = mx if m is None else jnp.maximum(m, mx)
        denom = None
        pv = None
        for s, v_ref in zip(ss, v_refs):
            e = jnp.exp2(s - m)
            sm = jnp.sum(e, axis=-1, keepdims=True)
            denom = sm if denom is None else denom + sm
            part = jnp.dot(e.astype(BF16), v_ref[...], preferred_element_type=F32)
            pv = part if pv is None else pv + part
        outs.append(pv * (1.0 / denom))
    o = outs[0] - lam * outs[1]
    y = o * lax.rsqrt(jnp.mean(o * o, axis=-1, keepdims=True) + NORM_EPS)
    o_ref[...] = ((y * g_ref[...]) * (1.0 - lam_init)).astype(o_ref.dtype)


def _diff_attention(cfg, qkv, lam_vecs, subln_g, layer_j, lam_init, need_ctx):
    d = cfg.d
    dh = HEAD_DIM
    hw = 2 * dh
    nh = d // hw
    tq = _tile(cfg.seq, 1024)
    qpb = cfg.seq // tq
    ctx_blk0 = cfg.t_lat // cfg.ctx_len
    lam_spec3 = pl.BlockSpec((None, 4, dh), lambda b, h, i: (layer_j, 0, 0))
    g_spec3 = pl.BlockSpec((None, 1, hw), lambda b, h, i: (layer_j, 0, 0))
    o_lat = pl.pallas_call(
        functools.partial(_diff_kernel, lam_init=lam_init, n_parts=2),
        grid=(cfg.batch, nh, qpb),
        in_specs=[lam_spec3, g_spec3,
                  pl.BlockSpec((tq, hw), lambda b, h, i: (b * qpb + i, h)),
                  pl.BlockSpec((cfg.seq, hw), lambda b, h, i: (b, nh + h)),
                  pl.BlockSpec((cfg.ctx_len, hw), lambda b, h, i: (ctx_blk0 + b, nh + h)),
                  pl.BlockSpec((cfg.seq, hw), lambda b, h, i: (b, 2 * nh + h)),
                  pl.BlockSpec((cfg.ctx_len, hw), lambda b, h, i: (ctx_blk0 + b, 2 * nh + h))],
        out_specs=pl.BlockSpec((tq, hw), lambda b, h, i: (b * qpb + i, h)),
        out_shape=jax.ShapeDtypeStruct((cfg.t_lat, d), BF16),
        compiler_params=_params("parallel", "parallel", "arbitrary"),
        name="diff_latent",
    )(lam_vecs, subln_g, qkv, qkv, qkv, qkv, qkv)
    if not need_ctx:
        return o_lat, None
    o_ctx = pl.pallas_call(
        functools.partial(_diff_kernel, lam_init=lam_init, n_parts=1),
        grid=(cfg.batch, nh),
        in_specs=[pl.BlockSpec((None, 4, dh), lambda b, h: (layer_j, 0, 0)),
                  pl.BlockSpec((None, 1, hw), lambda b, h: (layer_j, 0, 0)),
                  pl.BlockSpec((cfg.ctx_len, hw), lambda b, h: (ctx_blk0 + b, h)),
                  pl.BlockSpec((cfg.ctx_len, hw), lambda b, h: (ctx_blk0 + b, nh + h)),
                  pl.BlockSpec((cfg.ctx_len, hw), lambda b, h: (ctx_blk0 + b, 2 * nh + h))],
        out_specs=pl.BlockSpec((cfg.ctx_len, hw), lambda b, h: (b, h)),
        out_shape=jax.ShapeDtypeStruct((cfg.t_ctx, d), BF16),
        compiler_params=_params("parallel", "parallel"),
        name="diff_context",
    )(lam_vecs, subln_g, qkv, qkv, qkv)
    return o_lat, o_ctx


def _conv_kernel(gb_ref, gc_ref, u_ref, gcp_ref, up_ref, gcn_ref, un_ref, w_ref, o_ref,
                 *, blocks_per_seq, lat_blocks):
    i = pl.program_id(1)
    rb = gc_ref.shape[0]
    v = gc_ref[...].astype(F32) * u_ref[...].astype(F32)
    pos = jnp.where(i < lat_blocks, i % blocks_per_seq, 0)
    last = jnp.where(i < lat_blocks, blocks_per_seq - 1, 0)
    prev_row = jnp.where(pos > 0, gcp_ref[7:8, :].astype(F32) * up_ref[7:8, :].astype(F32), 0.0)
    next_row = jnp.where(pos < last, gcn_ref[0:1, :].astype(F32) * un_ref[0:1, :].astype(F32), 0.0)
    row = lax.broadcasted_iota(I32, v.shape, 0)
    v_prev = jnp.where(row == 0, prev_row, pltpu.roll(v, 1, 0))
    v_next = jnp.where(row == rb - 1, next_row, pltpu.roll(v, rb - 1, 0))
    w = w_ref[...]
    z = w[0:1] * v_prev + w[1:2] * v + w[2:3] * v_next
    o_ref[...] = (gb_ref[...].astype(F32) * z).astype(o_ref.dtype)


def _conv_gate(cfg, proj, conv_w, layer_j):
    t = proj.shape[0]
    d = cfg.d
    rb = _tile(cfg.ctx_len, 256)
    cb = _tile(d, 1024)
    ncb = d // cb
    sub = rb // 8
    last8 = t // 8 - 1
    blocks_per_seq = cfg.seq // rb
    lat_blocks = cfg.t_lat // rb
    assert cfg.ctx_len == rb, "context sequences must be exactly one row block"

    def main(part):
        return pl.BlockSpec((rb, cb), lambda j, i: (i, part * ncb + j))

    def halo(part, nxt):
        if nxt:
            return pl.BlockSpec((8, cb), lambda j, i: (jnp.minimum((i + 1) * sub, last8), part * ncb + j))
        return pl.BlockSpec((8, cb), lambda j, i: (jnp.maximum(i * sub - 1, 0), part * ncb + j))

    return pl.pallas_call(
        functools.partial(_conv_kernel, blocks_per_seq=blocks_per_seq, lat_blocks=lat_blocks),
        grid=(ncb, t // rb),
        in_specs=[main(0), main(1), main(2), halo(1, False), halo(2, False),
                  halo(1, True), halo(2, True),
                  pl.BlockSpec((None, CONV_W, cb), lambda j, i: (layer_j, 0, j))],
        out_specs=pl.BlockSpec((rb, cb), lambda j, i: (i, j)),
        out_shape=jax.ShapeDtypeStruct((t, d), BF16),
        compiler_params=_params("parallel", "parallel"),
        name="conv_gate",
    )(proj, proj, proj, proj, proj, proj, proj, conv_w)


def _route_kernel(logit_ref, bias_ref, eidx_ref, wts_ref, rank_ref, cnt_ref, run_ref):
    @pl.when(pl.program_id(0) == 0)
    def _():
        run_ref[...] = jnp.zeros_like(run_ref)

    ne, cols = logit_ref.shape
    pg = ne // N_GROUPS
    shape3 = (N_GROUPS, pg, cols)
    scores = jax.nn.sigmoid(logit_ref[...])
    s3 = scores.reshape(shape3)
    b3 = (scores + bias_ref[...]).reshape(shape3)
    sub = lax.broadcasted_iota(I32, shape3, 1).astype(F32)
    eid3 = (lax.broadcasted_iota(I32, shape3, 0) * pg).astype(F32) + sub
    neg = jnp.float32(-jnp.inf)

    def over_experts(fn, v3):
        return fn(fn(v3, axis=0), axis=0, keepdims=True)

    m1 = jnp.max(b3, axis=1, keepdims=True)
    i1 = jnp.min(jnp.where(b3 == m1, sub, float(pg)), axis=1, keepdims=True)
    m2 = jnp.max(jnp.where(sub == i1, neg, b3), axis=1, keepdims=True)
    gs = (m1 + m2).reshape(N_GROUPS, cols)
    gi = lax.broadcasted_iota(I32, (N_GROUPS, cols), 0)
    beaten = jnp.zeros((N_GROUPS, cols), I32)
    for h in range(N_GROUPS):
        gh = gs[h:h + 1, :]
        wins = (gh > gs) | ((gh == gs) & (gi > h))
        beaten = beaten + wins.astype(I32)
    keep = (beaten < TOPK_GROUPS).reshape(N_GROUPS, 1, cols)
    cand = jnp.where(keep, b3, MASK_VALUE)
    hits, ids, ws = [], [], []
    for k in range(TOP_K):
        m = over_experts(jnp.max, cand)
        idx = over_experts(jnp.min, jnp.where(cand == m, eid3, float(ne)))
        hit = eid3 == idx
        ws.append(over_experts(jnp.sum, jnp.where(hit, s3, 0.0)))
        ids.append(idx)
        hits.append(hit)
        cand = jnp.where(hit, neg, cand)
    wsel = jnp.concatenate(ws, axis=0)
    total = jnp.sum(wsel, axis=0, keepdims=True)
    eidx_ref[...] = jnp.concatenate(ids, axis=0).astype(I32)
    wts_ref[...] = wsel / total * ROUTED_SCALE
    chosen = hits[0]
    for hit in hits[1:]:
        chosen = chosen | hit
    chosen_f = jnp.where(chosen, 1.0, 0.0).reshape(ne, cols)
    earlier = (lax.broadcasted_iota(I32, (cols, cols), 0)
               < lax.broadcasted_iota(I32, (cols, cols), 1))
    local = jnp.dot(chosen_f.astype(BF16), jnp.where(earlier, 1.0, 0.0).astype(BF16),
                    preferred_element_type=F32)
    rank3 = (local + run_ref[...]).reshape(shape3)
    ranks = [over_experts(jnp.sum, jnp.where(hit, rank3, 0.0)) for hit in hits]
    rank_ref[...] = jnp.concatenate(ranks, axis=0).astype(I32)
    run_ref[...] = run_ref[...] + jnp.sum(chosen_f, axis=1, keepdims=True)
    cnt_ref[...] = run_ref[...]


def _route(logits_t, router_b, layer):
    ne, t = logits_t.shape
    cb = _tile(t, 256)
    kt_spec = pl.BlockSpec((TOP_K, cb), lambda i: (0, i))
    return pl.pallas_call(
        _route_kernel,
        grid=(t // cb,),
        in_specs=[pl.BlockSpec((ne, cb), lambda i: (0, i)),
                  pl.BlockSpec((None, ne, 1), lambda i: (layer, 0, 0))],
        out_specs=[kt_spec, kt_spec, kt_spec, pl.BlockSpec((ne, 1), lambda i: (0, 0))],
        out_shape=[jax.ShapeDtypeStruct((TOP_K, t), I32),
                   jax.ShapeDtypeStruct((TOP_K, t), F32),
                   jax.ShapeDtypeStruct((TOP_K, t), I32),
                   jax.ShapeDtypeStruct((ne, 1), F32)],
        scratch_shapes=[pltpu.VMEM((ne, 1), F32)],
        compiler_params=_params("arbitrary"),
        name="route",
    )(logits_t, router_b)


def _dispatch_plan(eidx, rank, counts_f, tile):
    ne = counts_f.shape[0]
    n_rows = eidx.shape[1] * TOP_K
    n_tiles = n_rows // tile
    n_work = n_tiles + ne
    counts = counts_f.reshape(ne).astype(I32)
    ends = jnp.cumsum(counts)
    offs = ends - counts
    e_ids = jnp.arange(ne, dtype=I32)
    pos = (rank + jnp.sum(jnp.where(eidx[..., None] == e_ids, offs, 0), axis=-1)).T
    first_tile = offs // tile
    last_tile = jnp.maximum(ends - 1, offs) // tile
    n_items = jnp.where(counts > 0, last_tile - first_tile + 1, 0)
    item_end = jnp.cumsum(n_items)
    item_off = item_end - n_items
    total = item_end[-1]
    w = jnp.arange(n_work, dtype=I32)
    e_of = jnp.minimum(jnp.sum((item_end[None, :] <= w[:, None]).astype(I32), axis=1), ne - 1)
    last_e = jnp.max(jnp.where(counts > 0, e_ids, 0)).astype(I32)
    live = w < total
    e_of = jnp.where(live, e_of, last_e)
    sel = e_of[:, None] == e_ids

    def at_e(table):
        return jnp.sum(jnp.where(sel, table, 0), axis=1)

    tile_of = jnp.where(live, at_e(first_tile) + (w - at_e(item_off)), n_tiles - 1).astype(I32)
    lo = jnp.clip(at_e(offs) - tile_of * tile, 0, tile)
    hi = jnp.clip(at_e(ends) - tile_of * tile, 0, tile)
    lo = jnp.where(live, lo, 0).astype(I32)
    hi = jnp.where(live, hi, 0).astype(I32)
    prev_tile = jnp.concatenate([jnp.full((1,), -1, I32), tile_of[:-1]])
    prev_e = jnp.concatenate([jnp.full((1,), -1, I32), e_of[:-1]])
    new_tile = (tile_of != prev_tile).astype(I32)
    new_e = (e_of != prev_e).astype(I32)
    return pos, (tile_of, e_of, lo, hi, new_tile, new_e)


def _dispatch_kernel(pos_ref, f_ref, xs_ref, sem):
    groups = f_ref.shape[0]

    def issue(g, carry):
        for s in range(SUBLANES):
            for k in range(TOP_K):
                p = pos_ref[0, (g * SUBLANES + s) * TOP_K + k]
                pltpu.make_async_copy(f_ref.at[g, pl.ds(s, 1)], xs_ref.at[pl.ds(p, 1)],
                                      sem).start(priority=k % 2)
        return carry

    lax.fori_loop(0, groups, issue, 0)
    rows = groups * SUBLANES
    for k in range(TOP_K):
        pltpu.make_async_copy(xs_ref.at[pl.ds(0, rows)], xs_ref.at[pl.ds(0, rows)], sem).wait()


def _dispatch(fp, pos):
    t, dw = fp.shape
    rb = _tile(t, 256)
    nb = t // rb
    return pl.pallas_call(
        _dispatch_kernel,
        grid=(nb,),
        in_specs=[pl.BlockSpec((None, 1, rb * TOP_K), lambda i: (i, 0, 0), memory_space=pltpu.SMEM),
                  pl.BlockSpec((rb // SUBLANES, SUBLANES, dw), lambda i: (i, 0, 0))],
        out_specs=pl.BlockSpec(memory_space=pl.ANY),
        out_shape=jax.ShapeDtypeStruct((t * TOP_K, dw), U32),
        scratch_shapes=[pltpu.SemaphoreType.DMA(())],
        compiler_params=_params("arbitrary"),
        name="moe_dispatch",
    )(pos.reshape(nb, 1, rb * TOP_K), fp.reshape(t // SUBLANES, SUBLANES, dw))


def _row_mask(lo_ref, hi_ref, w, rows):
    r = lax.broadcasted_iota(I32, (rows, 1), 0)
    return (r >= lo_ref[w]) & (r < hi_ref[w])


def _gmm_up_kernel(tile_ref, e_ref, lo_ref, hi_ref, nt_ref, ne_ref,
                   x_ref, wgt_ref, wut_ref, o_ref, wcat_ref):
    w = pl.program_id(0)
    ff = wgt_ref.shape[0]

    @pl.when(ne_ref[w] == 1)
    def _():
        wcat_ref[0:ff] = wgt_ref[...].astype(BF16)
        wcat_ref[ff:2 * ff] = wut_ref[...].astype(BF16)

    lo, hi = _unpack_bf16_pair(x_ref[...])
    lo, hi = lo.astype(BF16), hi.astype(BF16)
    rows, half = lo.shape
    ht = (lax.dot_general(wcat_ref[:, :half], lo, _NT, preferred_element_type=F32)
          + lax.dot_general(wcat_ref[:, half:], hi, _NT, preferred_element_type=F32))
    act_t = _silu(ht[:ff]) * ht[ff:]
    pad = (-ff) % LANE
    if pad:
        act_t = jnp.concatenate([act_t, jnp.zeros((pad, rows), F32)], axis=0)
    act = act_t.T[:, :ff].astype(o_ref.dtype)
    m = _row_mask(lo_ref, hi_ref, w, act.shape[0])

    @pl.when(nt_ref[w] == 1)
    def _():
        o_ref[...] = jnp.where(m, act, jnp.zeros_like(act))

    @pl.when(nt_ref[w] == 0)
    def _():
        o_ref[...] = jnp.where(m, act, o_ref[...])


def _gmm_down_kernel(tile_ref, e_ref, lo_ref, hi_ref, nt_ref, ne_ref,
                     h_ref, wd_ref, o_ref, wdb_ref):
    w = pl.program_id(0)

    @pl.when(ne_ref[w] == 1)
    def _():
        wdb_ref[...] = wd_ref[...].astype(BF16)

    y = jnp.dot(h_ref[...], wdb_ref[...], preferred_element_type=F32)
    half = y.shape[1] // 2
    yp = _pack_bf16_pair(y[:, :half], y[:, half:])
    m = _row_mask(lo_ref, hi_ref, w, yp.shape[0])

    @pl.when(nt_ref[w] == 1)
    def _():
        o_ref[...] = jnp.where(m, yp, jnp.zeros_like(yp))

    @pl.when(nt_ref[w] == 0)
    def _():
        o_ref[...] = jnp.where(m, yp, o_ref[...])


def _gmm_up(xs_sorted, plan, exp_gate_t, exp_up_t, layer):
    n_rows, dw = xs_sorted.shape
    ff, d = exp_gate_t.shape[2], exp_gate_t.shape[3]
    n_work = plan[0].shape[0]
    w_spec = pl.BlockSpec((None, None, ff, d), lambda w, tl, e, *_: (layer, e[w], 0, 0))
    return pl.pallas_call(
        _gmm_up_kernel,
        grid_spec=pltpu.PrefetchScalarGridSpec(
            num_scalar_prefetch=6,
            grid=(n_work,),
            in_specs=[pl.BlockSpec((GMM_TILE, dw), lambda w, tl, *_: (tl[w], 0)), w_spec, w_spec],
            out_specs=pl.BlockSpec((GMM_TILE, ff), lambda w, tl, *_: (tl[w], 0)),
            scratch_shapes=[pltpu.VMEM((2 * ff, d), BF16)],
        ),
        out_shape=jax.ShapeDtypeStruct((n_rows, ff), BF16),
        compiler_params=_params("arbitrary"),
        name="moe_up",
    )(*plan, xs_sorted, exp_gate_t, exp_up_t)


def _gmm_down(h_sorted, plan, exp_down, layer):
    n_rows, ff = h_sorted.shape
    d = exp_down.shape[3]
    n_work = plan[0].shape[0]
    return pl.pallas_call(
        _gmm_down_kernel,
        grid_spec=pltpu.PrefetchScalarGridSpec(
            num_scalar_prefetch=6,
            grid=(n_work,),
            in_specs=[pl.BlockSpec((GMM_TILE, ff), lambda w, tl, *_: (tl[w], 0)),
                      pl.BlockSpec((None, None, ff, d), lambda w, tl, e, *_: (layer, e[w], 0, 0))],
            out_specs=pl.BlockSpec((GMM_TILE, d // 2), lambda w, tl, *_: (tl[w], 0)),
            scratch_shapes=[pltpu.VMEM((ff, d), BF16)],
        ),
        out_shape=jax.ShapeDtypeStruct((n_rows, d // 2), U32),
        compiler_params=_params("arbitrary"),
        name="moe_down",
    )(*plan, h_sorted, exp_down)


def _combine_gather(pos_ref, ys_ref, buf_ref, sem, slot):
    groups = buf_ref.shape[2]

    def issue(g, carry):
        _combine_gather_group(pos_ref, ys_ref, buf_ref, sem, slot, g)
        return carry

    lax.fori_loop(0, groups, issue, 0)


def _combine_gather_group(pos_ref, ys_ref, buf_ref, sem, slot, g):
    for s in range(SUBLANES):
        for k in range(TOP_K):
            p = pos_ref[0, (g * SUBLANES + s) * TOP_K + k]
            pltpu.make_async_copy(ys_ref.at[pl.ds(p, 1)], buf_ref.at[slot, k, g, pl.ds(s, 1)],
                                  sem.at[slot]).start(priority=k % 2)


def _combine_wait(ys_ref, sem, slot, rows):
    for k in range(TOP_K):
        pltpu.make_async_copy(ys_ref.at[pl.ds(0, rows)], ys_ref.at[pl.ds(0, rows)],
                              sem.at[slot]).wait()


def _combine_moe(pos_ref, posn_ref, wts_ref, ys_ref, x_ref, hsh_ref, sd_ref, g2_ref,
                 xo_ref, buf_ref, sem):
    i = pl.program_id(0)
    groups, width = buf_ref.shape[2], buf_ref.shape[4]
    rows = groups * SUBLANES
    slot = i % 2

    @pl.when(i == 0)
    def _():
        _combine_gather(pos_ref, ys_ref, buf_ref, sem, 0)

    wts = wts_ref[...]
    wcols = [jnp.broadcast_to(wts[:, k:k + 1], (rows, LANE)) for k in range(TOP_K)]
    hsh = hsh_ref[...]
    _combine_wait(ys_ref, sem, slot, rows)
    ssq = jnp.zeros((rows, 1), F32)
    n_chunks = width // LANE
    groups_per_chunk = -(-groups // n_chunks)
    for c in range(n_chunks):
        for g in range(c * groups_per_chunk, min((c + 1) * groups_per_chunk, groups)):
            _combine_gather_group(posn_ref, ys_ref, buf_ref, sem, 1 - slot, g)
        lo_sl = slice(c * LANE, (c + 1) * LANE)
        hi_sl = slice(width + c * LANE, width + (c + 1) * LANE)
        acc_lo = jnp.dot(hsh, sd_ref[:, lo_sl], preferred_element_type=F32)
        acc_hi = jnp.dot(hsh, sd_ref[:, hi_sl], preferred_element_type=F32)
        for k in range(TOP_K):
            u = buf_ref[slot, k, :, :, lo_sl].reshape(rows, LANE)
            lo, hi = _unpack_bf16_pair(u)
            acc_lo = acc_lo + wcols[k] * lo
            acc_hi = acc_hi + wcols[k] * hi
        for sl, acc in ((lo_sl, acc_lo), (hi_sl, acc_hi)):
            xn = x_ref[:, sl] + g2_ref[:, sl] * acc
            xo_ref[:, sl] = xn
            ssq = ssq + jnp.sum(xn * xn, axis=-1, keepdims=True)

    @pl.when(i == pl.num_programs(0) - 1)
    def _():
        _combine_wait(ys_ref, sem, 1 - slot, rows)

    return ssq


def _combine_kernel(pos_ref, posn_ref, wts_ref, ys_ref, x_ref, hsh_ref, sd_ref, g2_ref,
                    ng_ref, sc_ref, sh_ref, xo_ref, a_ref, buf_ref, sem):
    ssq = _combine_moe(pos_ref, posn_ref, wts_ref, ys_ref, x_ref, hsh_ref, sd_ref, g2_ref,
                       xo_ref, buf_ref, sem)
    inv = lax.rsqrt(ssq * (1.0 / x_ref.shape[1]) + NORM_EPS)
    y = (xo_ref[...] * inv) * ng_ref[...]
    a_ref[...] = (y * (1.0 + sc_ref[...]) + sh_ref[...]).astype(a_ref.dtype)


def _combine_final_kernel(pos_ref, posn_ref, wts_ref, ys_ref, x_ref, hsh_ref, sd_ref, g2_ref,
                          ng_ref, o_ref, xn_ref, buf_ref, sem):
    ssq = _combine_moe(pos_ref, posn_ref, wts_ref, ys_ref, x_ref, hsh_ref, sd_ref, g2_ref,
                       xn_ref, buf_ref, sem)
    inv = lax.rsqrt(ssq * (1.0 / x_ref.shape[1]) + NORM_EPS)
    o_ref[...] = (xn_ref[...] * inv) * ng_ref[...]


def _combine(cfg, xs, ys_sorted, pos, wts, hsh, sh_down_bf, mod, layer, next_g, final):
    t, d = xs.shape
    ff = hsh.shape[1]
    rb = _tile(cfg.ctx_len, 128)
    nb = t // rb
    pos3 = pos.reshape(nb, 1, rb * TOP_K)
    row_spec = pl.BlockSpec((rb, d), lambda i: (i, 0))
    in_specs = [pl.BlockSpec((None, 1, rb * TOP_K), lambda i: (i, 0, 0), memory_space=pltpu.SMEM),
                pl.BlockSpec((None, 1, rb * TOP_K), lambda i: (jnp.minimum(i + 1, nb - 1), 0, 0),
                             memory_space=pltpu.SMEM),
                pl.BlockSpec((rb, TOP_K), lambda i: (i, 0)),
                pl.BlockSpec(memory_space=pl.ANY),
                row_spec,
                pl.BlockSpec((rb, ff), lambda i: (i, 0)),
                pl.BlockSpec((None, ff, d), lambda i: (layer, 0, 0)),
                _mod_spec(cfg, layer, 5, rb, d)]
    scratch = [pltpu.VMEM((2, TOP_K, rb // SUBLANES, SUBLANES, d // 2), U32),
               pltpu.SemaphoreType.DMA((2,))]
    args = (pos3, pos3, wts, ys_sorted, xs, hsh, sh_down_bf, mod)
    if final:
        return pl.pallas_call(
            _combine_final_kernel,
            grid=(nb,),
            in_specs=in_specs + [pl.BlockSpec((1, d), lambda i: (0, 0))],
            out_specs=row_spec,
            out_shape=jax.ShapeDtypeStruct((t, d), F32),
            scratch_shapes=[pltpu.VMEM((rb, d), F32)] + scratch,
            compiler_params=_params("arbitrary"),
            name="moe_combine_final",
        )(*args, next_g)
    nxt = layer + 1
    return pl.pallas_call(
        _combine_kernel,
        grid=(nb,),
        in_specs=in_specs + [pl.BlockSpec((None, 1, d), lambda i: (nxt, 0, 0)),
                             _mod_spec(cfg, nxt, 1, rb, d), _mod_spec(cfg, nxt, 0, rb, d)],
        out_specs=[row_spec, row_spec],
        out_shape=[jax.ShapeDtypeStruct((t, d), F32), jax.ShapeDtypeStruct((t, d), BF16)],
        scratch_shapes=scratch,
        compiler_params=_params("arbitrary"),
        name="moe_combine",
    )(*args, next_g, mod, mod)


def _rope_tables(seq):
    rows = seq // GRID_W
    row = jnp.broadcast_to(jnp.arange(rows)[:, None], (rows, GRID_W)).reshape(-1).astype(F32)
    col = jnp.broadcast_to(jnp.arange(GRID_W)[None, :], (rows, GRID_W)).reshape(-1).astype(F32)
    n_freq = HEAD_DIM // 4
    inv = ROPE_THETA ** (-jnp.arange(n_freq, dtype=F32) / n_freq)
    ang = jnp.concatenate([row[:, None] * inv, col[:, None] * inv], axis=-1)
    cos, sin = jnp.cos(ang), jnp.sin(ang)
    return jnp.concatenate([cos, cos], axis=-1), jnp.concatenate([-sin, sin], axis=-1)


def kernel(x, c, ctx, c_ctx, ada_down, ada_up, ada_b, norm1_g, norm2_g, a_w_qkv, a_w_o, a_sink,
           b_w_qkv, b_w_o, b_lam, b_subln_g, c_w_in, c_conv, c_w_out, router_w, router_b,
           exp_gate, exp_up, exp_down, sh_gate, sh_up, sh_down, final_g):
    batch, seq, d = x.shape
    ctx_len = ctx.shape[1]
    depth = ada_down.shape[0]
    cfg = _Cfg(batch, seq, ctx_len, d)
    assert batch + 1 <= MOD_ROWS

    xs = jnp.concatenate([x.reshape(-1, d), ctx.reshape(-1, d)], axis=0)
    src = jnp.concatenate([c, c_ctx[None, :], jnp.zeros((MOD_ROWS - batch - 1, d), F32)], axis=0)
    mod = _ada_table(src, ada_down, ada_up, ada_b).reshape(depth * MOD_ROWS, 1, 6 * d)
    cos_t, sin_t = _rope_tables(seq)
    norm1_g3 = norm1_g.reshape(depth, 1, d)
    norm2_g3 = norm2_g.reshape(depth, 1, d)
    router_b3 = router_b.reshape(depth, -1, 1)
    b_subln_g3 = b_subln_g.reshape(b_subln_g.shape[0], 1, -1)
    sh_gate_bf, sh_up_bf, sh_down_bf = (w.astype(BF16) for w in (sh_gate, sh_up, sh_down))
    router_hi = router_w.astype(BF16)
    router_lo = (router_w - router_hi.astype(F32)).astype(BF16)
    router_hl = jnp.concatenate([router_hi, router_lo], axis=-1)
    exp_gate_t, exp_up_t = jnp.swapaxes(exp_gate, 2, 3), jnp.swapaxes(exp_up, 2, 3)

    a = _norm1(cfg, xs, norm1_g3, mod, 0)
    out = None
    for i in range(depth):
        kind, j = i % N_MIXERS, i // N_MIXERS
        need_ctx = i < depth - 1
        if kind == 0:
            nq = d
            nk = d // KV_GROUP
            qkv = _proj(cfg, a, a_w_qkv, j, rope=(cos_t, sin_t, nq + nk))
            o_lat, o_ctx = _gqa_attention(cfg, qkv, a_sink, j, need_ctx)
            w_o = a_w_o
        elif kind == 1:
            lam_init = 0.8 - 0.6 * math.exp(-0.3 * i)
            qkv = _proj(cfg, a, b_w_qkv, j, rope=(cos_t, sin_t, 2 * d))
            o_lat, o_ctx = _diff_attention(cfg, qkv, b_lam, b_subln_g3, j, lam_init, need_ctx)
            w_o = b_w_o
        else:
            pr = _proj(cfg, a, c_w_in, j)
            o = _conv_gate(cfg, pr, c_conv, j)
            w_o = c_w_out
        if kind != 2:
            o = o_lat if o_ctx is None else jnp.concatenate([o_lat, o_ctx], axis=0)
        elif not need_ctx:
            o = o[:cfg.t_lat]
        xs = _proj_resid(cfg, o, w_o, j, xs, mod, i, 2)

        fp, logits, hsh = _norm2(cfg, xs, norm2_g3, mod, i, router_hl, sh_gate_bf, sh_up_bf)
        eidx_t, wts_t, rank_t, counts = _route(logits, router_b3, i)
        pos, plan = _dispatch_plan(eidx_t, rank_t, counts, GMM_TILE)
        wts = wts_t.T
        xs_sorted = _dispatch(fp, pos)
        h_sorted = _gmm_up(xs_sorted, plan, exp_gate_t, exp_up_t, i)
        ys_sorted = _gmm_down(h_sorted, plan, exp_down, i)
        if i == depth - 1:
            out = _combine(cfg, xs, ys_sorted, pos, wts, hsh, sh_down_bf, mod, i,
                           final_g.reshape(1, d), True)
        else:
            xs, a = _combine(cfg, xs, ys_sorted, pos, wts, hsh, sh_down_bf, mod, i, norm1_g3, False)
    return out.reshape(batch, seq, d)
```

```python
import functools
import math

import jax
import jax.numpy as jnp
from jax import lax
from jax.experimental import pallas as pl
from jax.experimental.pallas import tpu as pltpu

F32 = jnp.float32
BF16 = jnp.bfloat16
U32 = jnp.uint32
I32 = jnp.int32

HEAD_DIM = 128
GRID_W = 64
ROPE_THETA = 10000.0
NORM_EPS = 1e-6
MASK_VALUE = -1e30
WINDOW = 128
BLOCK = 128
CONV_W = 3
N_MIXERS = 3
N_EXPERTS = 64
TOP_K = 8
N_GROUPS = 8
TOPK_GROUPS = 4
ROUTED_SCALE = 2.5
KV_GROUP = 4
MOD_ROWS = 8

LANE = 128
SUBLANES = 8
VMEM_LIMIT_BYTES = 56 * 1024 * 1024
GMM_TILE = 256
DIFF_KEY_CHUNK = 512
HI_MASK = 0xFFFF0000
LOG2E = 1.4426950408889634
QUERY_SCALE = HEAD_DIM ** -0.5 * LOG2E


def _params(*sem):
    return pltpu.CompilerParams(dimension_semantics=sem, vmem_limit_bytes=VMEM_LIMIT_BYTES)


def _tile(n, pref):
    t = min(n, pref)
    while n % t:
        t //= 2
    return t


def _silu(v):
    return v * jax.nn.sigmoid(v)


def _pack_bf16_pair(lo, hi):
    lo_b = lax.bitcast_convert_type(lo.astype(BF16).astype(F32), U32)
    hi_b = lax.bitcast_convert_type(hi.astype(BF16).astype(F32), U32)
    return (lo_b >> 16) | (hi_b & jnp.uint32(HI_MASK))


def _unpack_bf16_pair(u):
    lo = lax.bitcast_convert_type(u << 16, F32)
    hi = lax.bitcast_convert_type(u & jnp.uint32(HI_MASK), F32)
    return lo, hi


def _ada_kernel(src_ref, down_ref, up_ref, b_ref, out_ref, t1_ref):
    @pl.when(pl.program_id(1) == 0)
    def _():
        s = _silu(src_ref[...]).astype(BF16)
        t1 = jnp.dot(s, down_ref[...].astype(BF16), preferred_element_type=F32)
        t1_ref[...] = t1.astype(BF16)

    out_ref[...] = jnp.dot(t1_ref[...], up_ref[...].astype(BF16),
                           preferred_element_type=F32) + b_ref[...]


def _ada_table(src, ada_down, ada_up, ada_b):
    depth, d, rank = ada_down.shape
    n6 = ada_up.shape[2]
    tn = _tile(n6, 2048)
    return pl.pallas_call(
        _ada_kernel,
        grid=(depth, n6 // tn),
        in_specs=[
            pl.BlockSpec((MOD_ROWS, d), lambda l, j: (0, 0)),
            pl.BlockSpec((None, d, rank), lambda l, j: (l, 0, 0)),
            pl.BlockSpec((None, rank, tn), lambda l, j: (l, 0, j)),
            pl.BlockSpec((None, 1, tn), lambda l, j: (l, 0, j)),
        ],
        out_specs=pl.BlockSpec((None, MOD_ROWS, tn), lambda l, j: (l, 0, j)),
        out_shape=jax.ShapeDtypeStruct((depth, MOD_ROWS, n6), F32),
        scratch_shapes=[pltpu.VMEM((MOD_ROWS, rank), BF16)],
        compiler_params=_params("arbitrary", "arbitrary"),
        name="ada_table",
    )(src, ada_down, ada_up, ada_b.reshape(depth, 1, n6))


def _norm_mod(xf, g, sc, sh):
    y = xf * lax.rsqrt(jnp.mean(xf * xf, axis=-1, keepdims=True) + NORM_EPS)
    return (y * g) * (1.0 + sc) + sh


class _Cfg:
    def __init__(self, batch, seq, ctx_len, d):
        self.batch, self.seq, self.ctx_len, self.d = batch, seq, ctx_len, d
        self.t_lat = batch * seq
        self.t_ctx = batch * ctx_len
        self.t = self.t_lat + self.t_ctx

    def group_of_rows(self, rows_per_block):
        per_seq = self.seq // rows_per_block
        assert per_seq * rows_per_block == self.seq
        assert self.t_lat % rows_per_block == 0 and self.t_ctx % rows_per_block == 0
        batch = self.batch
        return lambda i: jnp.minimum(i // per_seq, batch)


def _mod_spec(cfg, layer, chunk, rows_per_block, width, col_of=None):
    grp = cfg.group_of_rows(rows_per_block)
    per_chunk = cfg.d // width
    if col_of is None:
        return pl.BlockSpec((None, 1, width),
                            lambda i: (layer * MOD_ROWS + grp(i), 0, chunk * per_chunk))
    return pl.BlockSpec((None, 1, width),
                        lambda j, i: (layer * MOD_ROWS + grp(i), 0, chunk * per_chunk + j))


def _norm1_kernel(x_ref, g_ref, sc_ref, sh_ref, a_ref):
    a_ref[...] = _norm_mod(x_ref[...], g_ref[...], sc_ref[...], sh_ref[...]).astype(BF16)


def _norm1(cfg, xs, norm_g, mod, layer):
    t, d = xs.shape
    rb = _tile(cfg.ctx_len, 256)
    return pl.pallas_call(
        _norm1_kernel,
        grid=(t // rb,),
        in_specs=[
            pl.BlockSpec((rb, d), lambda i: (i, 0)),
            pl.BlockSpec((None, 1, d), lambda i: (layer, 0, 0)),
            _mod_spec(cfg, layer, 1, rb, d),
            _mod_spec(cfg, layer, 0, rb, d),
        ],
        out_specs=pl.BlockSpec((rb, d), lambda i: (i, 0)),
        out_shape=jax.ShapeDtypeStruct((t, d), BF16),
        compiler_params=_params("parallel"),
        name="norm1",
    )(xs, norm_g, mod, mod)


def _norm2_kernel(x_ref, g_ref, sc_ref, sh_ref, rw_ref, sg_ref, su_ref,
                  fp_ref, logit_ref, hsh_ref):
    f = _norm_mod(x_ref[...], g_ref[...], sc_ref[...], sh_ref[...])
    half = f.shape[1] // 2
    fp_ref[...] = _pack_bf16_pair(f[:, :half], f[:, half:])
    fb = f.astype(BF16)
    f_lo = (f - fb.astype(F32)).astype(BF16)
    ne = logit_ref.shape[0]
    both = (jnp.dot(fb, rw_ref[...], preferred_element_type=F32)
            + jnp.dot(f_lo, rw_ref[...], preferred_element_type=F32))
    both_t = both.T
    logit_ref[...] = both_t[:ne] + both_t[ne:]
    hg = jnp.dot(fb, sg_ref[...], preferred_element_type=F32)
    hu = jnp.dot(fb, su_ref[...], preferred_element_type=F32)
    hsh_ref[...] = (_silu(hg) * hu).astype(BF16)


def _norm2(cfg, xs, norm_g, mod, layer, router_w, sh_gate_bf, sh_up_bf):
    t, d = xs.shape
    ne = router_w.shape[2] // 2
    ff = sh_gate_bf.shape[2]
    rb = _tile(cfg.ctx_len, 256)
    return pl.pallas_call(
        _norm2_kernel,
        grid=(t // rb,),
        in_specs=[
            pl.BlockSpec((rb, d), lambda i: (i, 0)),
            pl.BlockSpec((None, 1, d), lambda i: (layer, 0, 0)),
            _mod_spec(cfg, layer, 4, rb, d),
            _mod_spec(cfg, layer, 3, rb, d),
            pl.BlockSpec((None, d, 2 * ne), lambda i: (layer, 0, 0)),
            pl.BlockSpec((None, d, ff), lambda i: (layer, 0, 0)),
            pl.BlockSpec((None, d, ff), lambda i: (layer, 0, 0)),
        ],
        out_specs=[
            pl.BlockSpec((rb, d // 2), lambda i: (i, 0)),
            pl.BlockSpec((ne, rb), lambda i: (0, i)),
            pl.BlockSpec((rb, ff), lambda i: (i, 0)),
        ],
        out_shape=[
            jax.ShapeDtypeStruct((t, d // 2), U32),
            jax.ShapeDtypeStruct((ne, t), F32),
            jax.ShapeDtypeStruct((t, ff), BF16),
        ],
        compiler_params=_params("parallel"),
        name="norm2_router",
    )(xs, norm_g, mod, mod, router_w, sh_gate_bf, sh_up_bf)


def _cast_weight(w_ref, wb_ref):
    @pl.when(pl.program_id(1) == 0)
    def _():
        wb_ref[...] = w_ref[...].astype(BF16)


def _proj_kernel(x_ref, w_ref, o_ref, wb_ref):
    _cast_weight(w_ref, wb_ref)
    o_ref[...] = jnp.dot(x_ref[...], wb_ref[...], preferred_element_type=F32).astype(o_ref.dtype)


def _proj_rope_kernel(x_ref, w_ref, cos_ref, sin_ref, o_ref, wb_ref, *, rope_cols, rope_rows, q_cols):
    _cast_weight(w_ref, wb_ref)
    acc = jnp.dot(x_ref[...], wb_ref[...], preferred_element_type=F32)
    acc = acc * jnp.where(pl.program_id(0) < q_cols, QUERY_SCALE, 1.0)
    do_rope = jnp.logical_and(pl.program_id(0) < rope_cols, pl.program_id(1) < rope_rows)

    @pl.when(do_rope)
    def _():
        c, s = cos_ref[...], sin_ref[...]
        for h in range(acc.shape[1] // HEAD_DIM):
            sl = slice(h * HEAD_DIM, (h + 1) * HEAD_DIM)
            xh = acc[:, sl]
            o_ref[:, sl] = (xh * c + pltpu.roll(xh, HEAD_DIM // 2, 1) * s).astype(o_ref.dtype)

    @pl.when(jnp.logical_not(do_rope))
    def _():
        o_ref[...] = acc.astype(o_ref.dtype)


def _proj_resid_kernel(x_ref, w_ref, res_ref, gate_ref, o_ref, wb_ref):
    _cast_weight(w_ref, wb_ref)
    acc = jnp.dot(x_ref[...], wb_ref[...], preferred_element_type=F32)
    o_ref[...] = res_ref[...] + gate_ref[...] * acc


def _proj_tiles(m, n, seq, n_align=0):
    return _tile(math.gcd(m, seq), 1024), _tile(math.gcd(n, n_align), 512)


def _proj(cfg, x, w, layer_j, rope=None):
    m, k = x.shape
    n = w.shape[2]
    tm, tn = _proj_tiles(m, n, cfg.seq, 0 if rope is None else rope[2])
    grid = (n // tn, m // tm)
    x_spec = pl.BlockSpec((tm, k), lambda j, i: (i, 0))
    w_spec = pl.BlockSpec((None, k, tn), lambda j, i: (layer_j, 0, j))
    o_spec = pl.BlockSpec((tm, tn), lambda j, i: (i, j))
    common = dict(
        grid=grid, out_specs=o_spec,
        out_shape=jax.ShapeDtypeStruct((m, n), BF16),
        scratch_shapes=[pltpu.VMEM((k, tn), BF16)],
        compiler_params=_params("arbitrary", "arbitrary"),
    )
    if rope is None:
        return pl.pallas_call(_proj_kernel, in_specs=[x_spec, w_spec], name="proj", **common)(x, w)
    cos_t, sin_t, rope_n = rope
    per_seq = cfg.seq // tm
    t_spec = pl.BlockSpec((tm, HEAD_DIM), lambda j, i: (i % per_seq, 0))
    assert cfg.d % tn == 0
    kern = functools.partial(_proj_rope_kernel, rope_cols=rope_n // tn, rope_rows=cfg.t_lat // tm,
                             q_cols=cfg.d // tn)
    return pl.pallas_call(kern, in_specs=[x_spec, w_spec, t_spec, t_spec],
                          name="proj_rope", **common)(x, w, cos_t, sin_t)


def _proj_resid(cfg, x, w, layer_j, res, mod, layer, gate_chunk):
    m, k = x.shape
    n = w.shape[2]
    tm, tn = _proj_tiles(m, n, cfg.seq)
    return pl.pallas_call(
        _proj_resid_kernel,
        grid=(n // tn, m // tm),
        in_specs=[
            pl.BlockSpec((tm, k), lambda j, i: (i, 0)),
            pl.BlockSpec((None, k, tn), lambda j, i: (layer_j, 0, j)),
            pl.BlockSpec((tm, tn), lambda j, i: (i, j)),
            _mod_spec(cfg, layer, gate_chunk, tm, tn, col_of=True),
        ],
        out_specs=pl.BlockSpec((tm, tn), lambda j, i: (i, j)),
        out_shape=jax.ShapeDtypeStruct((m, n), F32),
        scratch_shapes=[pltpu.VMEM((k, tn), BF16)],
        compiler_params=_params("arbitrary", "arbitrary"),
        name="proj_resid",
    )(x, w, res, mod)


_NT = (((1,), (1,)), ((), ()))


def _gqa_head_group(q_ref, k_parts, v_parts, valid, sink_ref, layer_j, kvh, o_ref):
    dh = HEAD_DIM
    rows = q_ref.shape[0]
    heads = [kvh * KV_GROUP + g for g in range(KV_GROUP)]
    qg = jnp.concatenate([q_ref[:, h * dh:(h + 1) * dh] for h in heads], axis=0)
    scores = []
    for idx, kp in enumerate(k_parts):
        s = lax.dot_general(qg, kp, _NT, preferred_element_type=F32)
        if idx == 0 and valid is not None:
            s = jnp.where(valid, s, MASK_VALUE)
        scores.append(s)
    row_head = lax.broadcasted_iota(I32, (KV_GROUP * rows, 1), 0) // rows
    sink_col = jnp.zeros((KV_GROUP * rows, 1), F32)
    for g, h in enumerate(heads):
        sink_col = jnp.where(row_head == g, sink_ref[layer_j, h] * LOG2E, sink_col)
    m = sink_col
    for s in scores:
        m = jnp.maximum(m, jnp.max(s, axis=-1, keepdims=True))
    es = [jnp.exp2(s - m) for s in scores]
    denom = jnp.exp2(sink_col - m)
    for e in es:
        denom = denom + jnp.sum(e, axis=-1, keepdims=True)
    o = None
    for e, vp in zip(es, v_parts):
        part = jnp.dot(e.astype(BF16), vp, preferred_element_type=F32)
        o = part if o is None else o + part
    o = o * (1.0 / denom)
    for g, h in enumerate(heads):
        o_ref[:, h * dh:(h + 1) * dh] = o[g * rows:(g + 1) * rows].astype(o_ref.dtype)


def _gqa_lat_kernel(sink_ref, q_ref, kp_ref, ko_ref, kn_ref, vp_ref, vo_ref, vn_ref,
                    kc_ref, vc_ref, o_ref, *, layer_j, n_kv, n_blk):
    n = pl.program_id(1)
    dh = HEAD_DIM
    shape = (KV_GROUP * BLOCK, 3 * BLOCK)
    qi = lax.broadcasted_iota(I32, shape, 0) % BLOCK
    c = lax.broadcasted_iota(I32, shape, 1)
    rel = (c - BLOCK) - qi
    kpos = n * BLOCK - BLOCK + c
    valid = (jnp.abs(rel) <= WINDOW) & (kpos >= 0) & (kpos < n_blk * BLOCK)
    for kvh in range(n_kv):
        sl = slice(kvh * dh, (kvh + 1) * dh)
        k_lat = jnp.concatenate([kp_ref[:, sl], ko_ref[:, sl], kn_ref[:, sl]], axis=0)
        v_lat = jnp.concatenate([vp_ref[:, sl], vo_ref[:, sl], vn_ref[:, sl]], axis=0)
        _gqa_head_group(q_ref, [k_lat, kc_ref[:, sl]], [v_lat, vc_ref[:, sl]],
                        valid, sink_ref, layer_j, kvh, o_ref)


def _gqa_ctx_kernel(sink_ref, q_ref, kc_ref, vc_ref, o_ref, *, layer_j, n_kv):
    dh = HEAD_DIM
    for kvh in range(n_kv):
        sl = slice(kvh * dh, (kvh + 1) * dh)
        _gqa_head_group(q_ref, [kc_ref[:, sl]], [vc_ref[:, sl]], None, sink_ref, layer_j, kvh, o_ref)


def _gqa_attention(cfg, qkv, sink, layer_j, need_ctx):
    t = qkv.shape[0]
    d = cfg.d
    dh = HEAD_DIM
    n_kv = d // dh // KV_GROUP
    nk = n_kv * dh
    n_blk = cfg.seq // BLOCK
    kcol, vcol = d // nk, d // nk + 1
    ctx_blk0 = cfg.t_lat // cfg.ctx_len
    smem = pl.BlockSpec(memory_space=pltpu.SMEM)

    def band(col, shift):
        return pl.BlockSpec(
            (BLOCK, nk),
            lambda b, n: (b * n_blk + jnp.clip(n + shift, 0, n_blk - 1), col))

    def ctx_spec(col):
        return pl.BlockSpec((cfg.ctx_len, nk), lambda b, n: (ctx_blk0 + b, col))

    o_lat = pl.pallas_call(
        functools.partial(_gqa_lat_kernel, layer_j=layer_j, n_kv=n_kv, n_blk=n_blk),
        grid=(cfg.batch, n_blk),
        in_specs=[smem, pl.BlockSpec((BLOCK, d), lambda b, n: (b * n_blk + n, 0)),
                  band(kcol, -1), band(kcol, 0), band(kcol, 1),
                  band(vcol, -1), band(vcol, 0), band(vcol, 1),
                  ctx_spec(kcol), ctx_spec(vcol)],
        out_specs=pl.BlockSpec((BLOCK, d), lambda b, n: (b * n_blk + n, 0)),
        out_shape=jax.ShapeDtypeStruct((cfg.t_lat, d), BF16),
        compiler_params=_params("parallel", "parallel"),
        name="gqa_latent",
    )(sink, qkv, qkv, qkv, qkv, qkv, qkv, qkv, qkv, qkv)
    if not need_ctx:
        return o_lat, None

    def ctx1(col):
        return pl.BlockSpec((cfg.ctx_len, nk), lambda b: (ctx_blk0 + b, col))

    o_ctx = pl.pallas_call(
        functools.partial(_gqa_ctx_kernel, layer_j=layer_j, n_kv=n_kv),
        grid=(cfg.batch,),
        in_specs=[smem, pl.BlockSpec((cfg.ctx_len, d), lambda b: (ctx_blk0 + b, 0)),
                  ctx1(kcol), ctx1(vcol)],
        out_specs=pl.BlockSpec((cfg.ctx_len, d), lambda b: (b, 0)),
        out_shape=jax.ShapeDtypeStruct((cfg.t_ctx, d), BF16),
        compiler_params=_params("parallel"),
        name="gqa_context",
    )(sink, qkv, qkv, qkv)
    return o_lat, o_ctx


def _diff_kernel(lam_ref, g_ref, q_ref, *refs, lam_init, n_parts):
    k_refs, v_refs, o_ref = refs[:n_parts], refs[n_parts:2 * n_parts], refs[2 * n_parts]
    dh = HEAD_DIM
    lf = lam_ref[...]
    lam = (jnp.exp(jnp.sum(lf[0:1] * lf[1:2], keepdims=True))
           - jnp.exp(jnp.sum(lf[2:3] * lf[3:4], keepdims=True)) + lam_init)
    outs = []
    for c in range(2):
        qc = q_ref[:, c * dh:(c + 1) * dh]
        m = denom = pv = None
        for k_ref, v_ref in zip(k_refs, v_refs):
            n_keys = k_ref.shape[0]
            for j in range(0, n_keys, DIFF_KEY_CHUNK):
                hi_j = min(j + DIFF_KEY_CHUNK, n_keys)
                s = lax.dot_general(qc, k_ref[j:hi_j, c * dh:(c + 1) * dh], _NT,
                                    preferred_element_type=F32)
                mx = jnp.max(s, axis=-1, keepdims=True)
                if m is None:
                    m = mx
                    e = jnp.exp2(s - m)
                    denom = jnp.sum(e, axis=-1, keepdims=True)
                    pv = jnp.dot(e.astype(BF16), v_ref[j:hi_j], preferred_element_type=F32)
                else:
                    m_new = jnp.maximum(m, mx)
                    alpha = jnp.exp2(m - m_new)
                    e = jnp.exp2(s - m_new)
                    denom = denom * alpha + jnp.sum(e, axis=-1, keepdims=True)
                    pv = pv * alpha + jnp.dot(e.astype(BF16), v_ref[j:hi_j],
                                              preferred_element_type=F32)
                    m = m_new
        outs.append(pv * (1.0 / denom))
    o = outs[0] - lam * outs[1]
    y = o * lax.rsqrt(jnp.mean(o * o, axis=-1, keepdims=True) + NORM_EPS)
    o_ref[...] = ((y * g_ref[...]) * (1.0 - lam_init)).astype(o_ref.dtype)


def _diff_attention(cfg, qkv, lam_vecs, subln_g, layer_j, lam_init, need_ctx):
    d = cfg.d
    dh = HEAD_DIM
    hw = 2 * dh
    nh = d // hw
    tq = _tile(cfg.seq, 1024)
    qpb = cfg.seq // tq
    ctx_blk0 = cfg.t_lat // cfg.ctx_len
    lam_spec3 = pl.BlockSpec((None, 4, dh), lambda b, h, i: (layer_j, 0, 0))
    g_spec3 = pl.BlockSpec((None, 1, hw), lambda b, h, i: (layer_j, 0, 0))
    o_lat = pl.pallas_call(
        functools.partial(_diff_kernel, lam_init=lam_init, n_parts=2),
        grid=(cfg.batch, nh, qpb),
        in_specs=[lam_spec3, g_spec3,
                  pl.BlockSpec((tq, hw), lambda b, h, i: (b * qpb + i, h)),
                  pl.BlockSpec((cfg.seq, hw), lambda b, h, i: (b, nh + h)),
                  pl.BlockSpec((cfg.ctx_len, hw), lambda b, h, i: (ctx_blk0 + b, nh + h)),
                  pl.BlockSpec((cfg.seq, hw), lambda b, h, i: (b, 2 * nh + h)),
                  pl.BlockSpec((cfg.ctx_len, hw), lambda b, h, i: (ctx_blk0 + b, 2 * nh + h))],
        out_specs=pl.BlockSpec((tq, hw), lambda b, h, i: (b * qpb + i, h)),
        out_shape=jax.ShapeDtypeStruct((cfg.t_lat, d), BF16),
        compiler_params=_params("parallel", "parallel", "arbitrary"),
        name="diff_latent",
    )(lam_vecs, subln_g, qkv, qkv, qkv, qkv, qkv)
    if not need_ctx:
        return o_lat, None
    o_ctx = pl.pallas_call(
        functools.partial(_diff_kernel, lam_init=lam_init, n_parts=1),
        grid=(cfg.batch, nh),
        in_specs=[pl.BlockSpec((None, 4, dh), lambda b, h: (layer_j, 0, 0)),
                  pl.BlockSpec((None, 1, hw), lambda b, h: (layer_j, 0, 0)),
                  pl.BlockSpec((cfg.ctx_len, hw), lambda b, h: (ctx_blk0 + b, h)),
                  pl.BlockSpec((cfg.ctx_len, hw), lambda b, h: (ctx_blk0 + b, nh + h)),
                  pl.BlockSpec((cfg.ctx_len, hw), lambda b, h: (ctx_blk0 + b, 2 * nh + h))],
        out_specs=pl.BlockSpec((cfg.ctx_len, hw), lambda b, h: (b, h)),
        out_shape=jax.ShapeDtypeStruct((cfg.t_ctx, d), BF16),
        compiler_params=_params("parallel", "parallel"),
        name="diff_context",
    )(lam_vecs, subln_g, qkv, qkv, qkv)
    return o_lat, o_ctx


def _conv_kernel(gb_ref, gc_ref, u_ref, gcp_ref, up_ref, gcn_ref, un_ref, w_ref, o_ref,
                 *, blocks_per_seq, lat_blocks):
    i = pl.program_id(1)
    rb = gc_ref.shape[0]
    v = gc_ref[...].astype(F32) * u_ref[...].astype(F32)
    pos = jnp.where(i < lat_blocks, i % blocks_per_seq, 0)
    last = jnp.where(i < lat_blocks, blocks_per_seq - 1, 0)
    prev_row = jnp.where(pos > 0, gcp_ref[7:8, :].astype(F32) * up_ref[7:8, :].astype(F32), 0.0)
    next_row = jnp.where(pos < last, gcn_ref[0:1, :].astype(F32) * un_ref[0:1, :].astype(F32), 0.0)
    row = lax.broadcasted_iota(I32, v.shape, 0)
    v_prev = jnp.where(row == 0, prev_row, pltpu.roll(v, 1, 0))
    v_next = jnp.where(row == rb - 1, next_row, pltpu.roll(v, rb - 1, 0))
    w = w_ref[...]
    z = w[0:1] * v_prev + w[1:2] * v + w[2:3] * v_next
    o_ref[...] = (gb_ref[...].astype(F32) * z).astype(o_ref.dtype)


def _conv_gate(cfg, proj, conv_w, layer_j):
    t = proj.shape[0]
    d = cfg.d
    rb = _tile(cfg.ctx_len, 256)
    cb = _tile(d, 1024)
    ncb = d // cb
    sub = rb // 8
    last8 = t // 8 - 1
    blocks_per_seq = cfg.seq // rb
    lat_blocks = cfg.t_lat // rb
    assert cfg.ctx_len == rb, "context sequences must be exactly one row block"

    def main(part):
        return pl.BlockSpec((rb, cb), lambda j, i: (i, part * ncb + j))

    def halo(part, nxt):
        if nxt:
            return pl.BlockSpec((8, cb), lambda j, i: (jnp.minimum((i + 1) * sub, last8), part * ncb + j))
        return pl.BlockSpec((8, cb), lambda j, i: (jnp.maximum(i * sub - 1, 0), part * ncb + j))

    return pl.pallas_call(
        functools.partial(_conv_kernel, blocks_per_seq=blocks_per_seq, lat_blocks=lat_blocks),
        grid=(ncb, t // rb),
        in_specs=[main(0), main(1), main(2), halo(1, False), halo(2, False),
                  halo(1, True), halo(2, True),
                  pl.BlockSpec((None, CONV_W, cb), lambda j, i: (layer_j, 0, j))],
        out_specs=pl.BlockSpec((rb, cb), lambda j, i: (i, j)),
        out_shape=jax.ShapeDtypeStruct((t, d), BF16),
        compiler_params=_params("parallel", "parallel"),
        name="conv_gate",
    )(proj, proj, proj, proj, proj, proj, proj, conv_w)


def _route_kernel(logit_ref, bias_ref, eidx_ref, wts_ref, rank_ref, cnt_ref, run_ref):
    @pl.when(pl.program_id(0) == 0)
    def _():
        run_ref[...] = jnp.zeros_like(run_ref)

    ne, cols = logit_ref.shape
    pg = ne // N_GROUPS
    shape3 = (N_GROUPS, pg, cols)
    scores = jax.nn.sigmoid(logit_ref[...])
    s3 = scores.reshape(shape3)
    b3 = (scores + bias_ref[...]).reshape(shape3)
    sub = lax.broadcasted_iota(I32, shape3, 1).astype(F32)
    eid3 = (lax.broadcasted_iota(I32, shape3, 0) * pg).astype(F32) + sub
    neg = jnp.float32(-jnp.inf)

    def over_experts(fn, v3):
        return fn(fn(v3, axis=0), axis=0, keepdims=True)

    m1 = jnp.max(b3, axis=1, keepdims=True)
    i1 = jnp.min(jnp.where(b3 == m1, sub, float(pg)), axis=1, keepdims=True)
    m2 = jnp.max(jnp.where(sub == i1, neg, b3), axis=1, keepdims=True)
    gs = (m1 + m2).reshape(N_GROUPS, cols)
    gi = lax.broadcasted_iota(I32, (N_GROUPS, cols), 0)
    beaten = jnp.zeros((N_GROUPS, cols), I32)
    for h in range(N_GROUPS):
        gh = gs[h:h + 1, :]
        wins = (gh > gs) | ((gh == gs) & (gi > h))
        beaten = beaten + wins.astype(I32)
    keep = (beaten < TOPK_GROUPS).reshape(N_GROUPS, 1, cols)
    cand = jnp.where(keep, b3, MASK_VALUE)
    hits, ids, ws = [], [], []
    for k in range(TOP_K):
        m = over_experts(jnp.max, cand)
        idx = over_experts(jnp.min, jnp.where(cand == m, eid3, float(ne)))
        hit = eid3 == idx
        ws.append(over_experts(jnp.sum, jnp.where(hit, s3, 0.0)))
        ids.append(idx)
        hits.append(hit)
        cand = jnp.where(hit, neg, cand)
    wsel = jnp.concatenate(ws, axis=0)
    total = jnp.sum(wsel, axis=0, keepdims=True)
    eidx_ref[...] = jnp.concatenate(ids, axis=0).astype(I32)
    wts_ref[...] = wsel / total * ROUTED_SCALE
    chosen = hits[0]
    for hit in hits[1:]:
        chosen = chosen | hit
    chosen_f = jnp.where(chosen, 1.0, 0.0).reshape(ne, cols)
    earlier = (lax.broadcasted_iota(I32, (cols, cols), 0)
               < lax.broadcasted_iota(I32, (cols, cols), 1))
    local = jnp.dot(chosen_f.astype(BF16), jnp.where(earlier, 1.0, 0.0).astype(BF16),
                    preferred_element_type=F32)
    rank3 = (local + run_ref[...]).reshape(shape3)
    ranks = [over_experts(jnp.sum, jnp.where(hit, rank3, 0.0)) for hit in hits]
    rank_ref[...] = jnp.concatenate(ranks, axis=0).astype(I32)
    run_ref[...] = run_ref[...] + jnp.sum(chosen_f, axis=1, keepdims=True)
    cnt_ref[...] = run_ref[...]


def _route(logits_t, router_b, layer):
    ne, t = logits_t.shape
    cb = _tile(t, 256)
    kt_spec = pl.BlockSpec((TOP_K, cb), lambda i: (0, i))
    return pl.pallas_call(
        _route_kernel,
        grid=(t // cb,),
        in_specs=[pl.BlockSpec((ne, cb), lambda i: (0, i)),
                  pl.BlockSpec((None, ne, 1), lambda i: (layer, 0, 0))],
        out_specs=[kt_spec, kt_spec, kt_spec, pl.BlockSpec((ne, 1), lambda i: (0, 0))],
        out_shape=[jax.ShapeDtypeStruct((TOP_K, t), I32),
                   jax.ShapeDtypeStruct((TOP_K, t), F32),
                   jax.ShapeDtypeStruct((TOP_K, t), I32),
                   jax.ShapeDtypeStruct((ne, 1), F32)],
        scratch_shapes=[pltpu.VMEM((ne, 1), F32)],
        compiler_params=_params("arbitrary"),
        name="route",
    )(logits_t, router_b)


def _dispatch_plan(eidx, rank, counts_f, tile):
    ne = counts_f.shape[0]
    n_rows = eidx.shape[1] * TOP_K
    n_tiles = n_rows // tile
    n_work = n_tiles + ne
    counts = counts_f.reshape(ne).astype(I32)
    ends = jnp.cumsum(counts)
    offs = ends - counts
    e_ids = jnp.arange(ne, dtype=I32)
    pos = (rank + jnp.sum(jnp.where(eidx[..., None] == e_ids, offs, 0), axis=-1)).T
    first_tile = offs // tile
    last_tile = jnp.maximum(ends - 1, offs) // tile
    n_items = jnp.where(counts > 0, last_tile - first_tile + 1, 0)
    item_end = jnp.cumsum(n_items)
    item_off = item_end - n_items
    total = item_end[-1]
    w = jnp.arange(n_work, dtype=I32)
    e_of = jnp.minimum(jnp.sum((item_end[None, :] <= w[:, None]).astype(I32), axis=1), ne - 1)
    last_e = jnp.max(jnp.where(counts > 0, e_ids, 0)).astype(I32)
    live = w < total
    e_of = jnp.where(live, e_of, last_e)
    sel = e_of[:, None] == e_ids

    def at_e(table):
        return jnp.sum(jnp.where(sel, table, 0), axis=1)

    tile_of = jnp.where(live, at_e(first_tile) + (w - at_e(item_off)), n_tiles - 1).astype(I32)
    lo = jnp.clip(at_e(offs) - tile_of * tile, 0, tile)
    hi = jnp.clip(at_e(ends) - tile_of * tile, 0, tile)
    lo = jnp.where(live, lo, 0).astype(I32)
    hi = jnp.where(live, hi, 0).astype(I32)
    prev_tile = jnp.concatenate([jnp.full((1,), -1, I32), tile_of[:-1]])
    prev_e = jnp.concatenate([jnp.full((1,), -1, I32), e_of[:-1]])
    new_tile = (tile_of != prev_tile).astype(I32)
    new_e = (e_of != prev_e).astype(I32)
    return pos, (tile_of, e_of, lo, hi, new_tile, new_e)


def _dispatch_kernel(pos_ref, f_ref, xs_ref, sem):
    groups = f_ref.shape[0]

    def issue(g, carry):
        for s in range(SUBLANES):
            for k in range(TOP_K):
                p = pos_ref[0, (g * SUBLANES + s) * TOP_K + k]
                pltpu.make_async_copy(f_ref.at[g, pl.ds(s, 1)], xs_ref.at[pl.ds(p, 1)],
                                      sem).start(priority=k % 2)
        return carry

    lax.fori_loop(0, groups, issue, 0)
    rows = groups * SUBLANES
    for k in range(TOP_K):
        pltpu.make_async_copy(xs_ref.at[pl.ds(0, rows)], xs_ref.at[pl.ds(0, rows)], sem).wait()


def _dispatch(fp, pos):
    t, dw = fp.shape
    rb = _tile(t, 256)
    nb = t // rb
    return pl.pallas_call(
        _dispatch_kernel,
        grid=(nb,),
        in_specs=[pl.BlockSpec((None, 1, rb * TOP_K), lambda i: (i, 0, 0), memory_space=pltpu.SMEM),
                  pl.BlockSpec((rb // SUBLANES, SUBLANES, dw), lambda i: (i, 0, 0))],
        out_specs=pl.BlockSpec(memory_space=pl.ANY),
        out_shape=jax.ShapeDtypeStruct((t * TOP_K, dw), U32),
        scratch_shapes=[pltpu.SemaphoreType.DMA(())],
        compiler_params=_params("arbitrary"),
        name="moe_dispatch",
    )(pos.reshape(nb, 1, rb * TOP_K), fp.reshape(t // SUBLANES, SUBLANES, dw))


def _row_mask(lo_ref, hi_ref, w, rows):
    r = lax.broadcasted_iota(I32, (rows, 1), 0)
    return (r >= lo_ref[w]) & (r < hi_ref[w])


def _gmm_up_kernel(tile_ref, e_ref, lo_ref, hi_ref, nt_ref, ne_ref,
                   x_ref, wgt_ref, wut_ref, o_ref, wcat_ref):
    w = pl.program_id(0)
    ff = wgt_ref.shape[0]

    @pl.when(ne_ref[w] == 1)
    def _():
        wcat_ref[0:ff] = wgt_ref[...].astype(BF16)
        wcat_ref[ff:2 * ff] = wut_ref[...].astype(BF16)

    lo, hi = _unpack_bf16_pair(x_ref[...])
    lo, hi = lo.astype(BF16), hi.astype(BF16)
    rows, half = lo.shape
    ht = (lax.dot_general(wcat_ref[:, :half], lo, _NT, preferred_element_type=F32)
          + lax.dot_general(wcat_ref[:, half:], hi, _NT, preferred_element_type=F32))
    act_t = _silu(ht[:ff]) * ht[ff:]
    pad = (-ff) % LANE
    if pad:
        act_t = jnp.concatenate([act_t, jnp.zeros((pad, rows), F32)], axis=0)
    act = act_t.T[:, :ff].astype(o_ref.dtype)
    m = _row_mask(lo_ref, hi_ref, w, act.shape[0])

    @pl.when(nt_ref[w] == 1)
    def _():
        o_ref[...] = jnp.where(m, act, jnp.zeros_like(act))

    @pl.when(nt_ref[w] == 0)
    def _():
        o_ref[...] = jnp.where(m, act, o_ref[...])


def _gmm_down_kernel(tile_ref, e_ref, lo_ref, hi_ref, nt_ref, ne_ref,
                     h_ref, wd_ref, o_ref, wdb_ref):
    w = pl.program_id(0)

    @pl.when(ne_ref[w] == 1)
    def _():
        wdb_ref[...] = wd_ref[...].astype(BF16)

    y = jnp.dot(h_ref[...], wdb_ref[...], preferred_element_type=F32)
    half = y.shape[1] // 2
    yp = _pack_bf16_pair(y[:, :half], y[:, half:])
    m = _row_mask(lo_ref, hi_ref, w, yp.shape[0])

    @pl.when(nt_ref[w] == 1)
    def _():
        o_ref[...] = jnp.where(m, yp, jnp.zeros_like(yp))

    @pl.when(nt_ref[w] == 0)
    def _():
        o_ref[...] = jnp.where(m, yp, o_ref[...])


def _gmm_up(xs_sorted, plan, exp_gate_t, exp_up_t, layer):
    n_rows, dw = xs_sorted.shape
    ff, d = exp_gate_t.shape[2], exp_gate_t.shape[3]
    n_work = plan[0].shape[0]
    w_spec = pl.BlockSpec((None, None, ff, d), lambda w, tl, e, *_: (layer, e[w], 0, 0))
    return pl.pallas_call(
        _gmm_up_kernel,
        grid_spec=pltpu.PrefetchScalarGridSpec(
            num_scalar_prefetch=6,
            grid=(n_work,),
            in_specs=[pl.BlockSpec((GMM_TILE, dw), lambda w, tl, *_: (tl[w], 0)), w_spec, w_spec],
            out_specs=pl.BlockSpec((GMM_TILE, ff), lambda w, tl, *_: (tl[w], 0)),
            scratch_shapes=[pltpu.VMEM((2 * ff, d), BF16)],
        ),
        out_shape=jax.ShapeDtypeStruct((n_rows, ff), BF16),
        compiler_params=_params("arbitrary"),
        name="moe_up",
    )(*plan, xs_sorted, exp_gate_t, exp_up_t)


def _gmm_down(h_sorted, plan, exp_down, layer):
    n_rows, ff = h_sorted.shape
    d = exp_down.shape[3]
    n_work = plan[0].shape[0]
    return pl.pallas_call(
        _gmm_down_kernel,
        grid_spec=pltpu.PrefetchScalarGridSpec(
            num_scalar_prefetch=6,
            grid=(n_work,),
            in_specs=[pl.BlockSpec((GMM_TILE, ff), lambda w, tl, *_: (tl[w], 0)),
                      pl.BlockSpec((None, None, ff, d), lambda w, tl, e, *_: (layer, e[w], 0, 0))],
            out_specs=pl.BlockSpec((GMM_TILE, d // 2), lambda w, tl, *_: (tl[w], 0)),
            scratch_shapes=[pltpu.VMEM((ff, d), BF16)],
        ),
        out_shape=jax.ShapeDtypeStruct((n_rows, d // 2), U32),
        compiler_params=_params("arbitrary"),
        name="moe_down",
    )(*plan, h_sorted, exp_down)


def _combine_gather(pos_ref, ys_ref, buf_ref, sem, slot):
    groups = buf_ref.shape[2]

    def issue(g, carry):
        _combine_gather_group(pos_ref, ys_ref, buf_ref, sem, slot, g)
        return carry

    lax.fori_loop(0, groups, issue, 0)


def _combine_gather_group(pos_ref, ys_ref, buf_ref, sem, slot, g):
    for s in range(SUBLANES):
        for k in range(TOP_K):
            p = pos_ref[0, (g * SUBLANES + s) * TOP_K + k]
            pltpu.make_async_copy(ys_ref.at[pl.ds(p, 1)], buf_ref.at[slot, k, g, pl.ds(s, 1)],
                                  sem.at[slot]).start(priority=k % 2)


def _combine_wait(ys_ref, sem, slot, rows):
    for k in range(TOP_K):
        pltpu.make_async_copy(ys_ref.at[pl.ds(0, rows)], ys_ref.at[pl.ds(0, rows)],
                              sem.at[slot]).wait()


def _combine_moe(pos_ref, posn_ref, wts_ref, ys_ref, x_ref, hsh_ref, sd_ref, g2_ref,
                 xo_ref, buf_ref, sem):
    i = pl.program_id(0)
    groups, width = buf_ref.shape[2], buf_ref.shape[4]
    rows = groups * SUBLANES
    slot = i % 2

    @pl.when(i == 0)
    def _():
        _combine_gather(pos_ref, ys_ref, buf_ref, sem, 0)

    wts = wts_ref[...]
    wcols = [jnp.broadcast_to(wts[:, k:k + 1], (rows, LANE)) for k in range(TOP_K)]
    hsh = hsh_ref[...]
    _combine_wait(ys_ref, sem, slot, rows)
    ssq = jnp.zeros((rows, 1), F32)
    n_chunks = width // LANE
    groups_per_chunk = -(-groups // n_chunks)
    for c in range(n_chunks):
        for g in range(c * groups_per_chunk, min((c + 1) * groups_per_chunk, groups)):
            _combine_gather_group(posn_ref, ys_ref, buf_ref, sem, 1 - slot, g)
        lo_sl = slice(c * LANE, (c + 1) * LANE)
        hi_sl = slice(width + c * LANE, width + (c + 1) * LANE)
        acc_lo = jnp.dot(hsh, sd_ref[:, lo_sl], preferred_element_type=F32)
        acc_hi = jnp.dot(hsh, sd_ref[:, hi_sl], preferred_element_type=F32)
        for k in range(TOP_K):
            u = buf_ref[slot, k, :, :, lo_sl].reshape(rows, LANE)
            lo, hi = _unpack_bf16_pair(u)
            acc_lo = acc_lo + wcols[k] * lo
            acc_hi = acc_hi + wcols[k] * hi
        for sl, acc in ((lo_sl, acc_lo), (hi_sl, acc_hi)):
            xn = x_ref[:, sl] + g2_ref[:, sl] * acc
            xo_ref[:, sl] = xn
            ssq = ssq + jnp.sum(xn * xn, axis=-1, keepdims=True)

    @pl.when(i == pl.num_programs(0) - 1)
    def _():
        _combine_wait(ys_ref, sem, 1 - slot, rows)

    return ssq


def _combine_kernel(pos_ref, posn_ref, wts_ref, ys_ref, x_ref, hsh_ref, sd_ref, g2_ref,
                    ng_ref, sc_ref, sh_ref, xo_ref, a_ref, buf_ref, sem):
    ssq = _combine_moe(pos_ref, posn_ref, wts_ref, ys_ref, x_ref, hsh_ref, sd_ref, g2_ref,
                       xo_ref, buf_ref, sem)
    inv = lax.rsqrt(ssq * (1.0 / x_ref.shape[1]) + NORM_EPS)
    y = (xo_ref[...] * inv) * ng_ref[...]
    a_ref[...] = (y * (1.0 + sc_ref[...]) + sh_ref[...]).astype(a_ref.dtype)


def _combine_final_kernel(pos_ref, posn_ref, wts_ref, ys_ref, x_ref, hsh_ref, sd_ref, g2_ref,
                          ng_ref, o_ref, xn_ref, buf_ref, sem):
    ssq = _combine_moe(pos_ref, posn_ref, wts_ref, ys_ref, x_ref, hsh_ref, sd_ref, g2_ref,
                       xn_ref, buf_ref, sem)
    inv = lax.rsqrt(ssq * (1.0 / x_ref.shape[1]) + NORM_EPS)
    o_ref[...] = (xn_ref[...] * inv) * ng_ref[...]


def _combine(cfg, xs, ys_sorted, pos, wts, hsh, sh_down_bf, mod, layer, next_g, final):
    t, d = xs.shape
    ff = hsh.shape[1]
    rb = _tile(cfg.ctx_len, 128)
    nb = t // rb
    pos3 = pos.reshape(nb, 1, rb * TOP_K)
    row_spec = pl.BlockSpec((rb, d), lambda i: (i, 0))
    in_specs = [pl.BlockSpec((None, 1, rb * TOP_K), lambda i: (i, 0, 0), memory_space=pltpu.SMEM),
                pl.BlockSpec((None, 1, rb * TOP_K), lambda i: (jnp.minimum(i + 1, nb - 1), 0, 0),
                             memory_space=pltpu.SMEM),
                pl.BlockSpec((rb, TOP_K), lambda i: (i, 0)),
                pl.BlockSpec(memory_space=pl.ANY),
                row_spec,
                pl.BlockSpec((rb, ff), lambda i: (i, 0)),
                pl.BlockSpec((None, ff, d), lambda i: (layer, 0, 0)),
                _mod_spec(cfg, layer, 5, rb, d)]
    scratch = [pltpu.VMEM((2, TOP_K, rb // SUBLANES, SUBLANES, d // 2), U32),
               pltpu.SemaphoreType.DMA((2,))]
    args = (pos3, pos3, wts, ys_sorted, xs, hsh, sh_down_bf, mod)
    if final:
        return pl.pallas_call(
            _combine_final_kernel,
            grid=(nb,),
            in_specs=in_specs + [pl.BlockSpec((1, d), lambda i: (0, 0))],
            out_specs=row_spec,
            out_shape=jax.ShapeDtypeStruct((t, d), F32),
            scratch_shapes=[pltpu.VMEM((rb, d), F32)] + scratch,
            compiler_params=_params("arbitrary"),
            name="moe_combine_final",
        )(*args, next_g)
    nxt = layer + 1
    return pl.pallas_call(
        _combine_kernel,
        grid=(nb,),
        in_specs=in_specs + [pl.BlockSpec((None, 1, d), lambda i: (nxt, 0, 0)),
                             _mod_spec(cfg, nxt, 1, rb, d), _mod_spec(cfg, nxt, 0, rb, d)],
        out_specs=[row_spec, row_spec],
        out_shape=[jax.ShapeDtypeStruct((t, d), F32), jax.ShapeDtypeStruct((t, d), BF16)],
        scratch_shapes=scratch,
        compiler_params=_params("arbitrary"),
        name="moe_combine",
    )(*args, next_g, mod, mod)


def _rope_tables(seq):
    rows = seq // GRID_W
    row = jnp.broadcast_to(jnp.arange(rows)[:, None], (rows, GRID_W)).reshape(-1).astype(F32)
    col = jnp.broadcast_to(jnp.arange(GRID_W)[None, :], (rows, GRID_W)).reshape(-1).astype(F32)
    n_freq = HEAD_DIM // 4
    inv = ROPE_THETA ** (-jnp.arange(n_freq, dtype=F32) / n_freq)
    ang = jnp.concatenate([row[:, None] * inv, col[:, None] * inv], axis=-1)
    cos, sin = jnp.cos(ang), jnp.sin(ang)
    return jnp.concatenate([cos, cos], axis=-1), jnp.concatenate([-sin, sin], axis=-1)


def kernel(x, c, ctx, c_ctx, ada_down, ada_up, ada_b, norm1_g, norm2_g, a_w_qkv, a_w_o, a_sink,
           b_w_qkv, b_w_o, b_lam, b_subln_g, c_w_in, c_conv, c_w_out, router_w, router_b,
           exp_gate, exp_up, exp_down, sh_gate, sh_up, sh_down, final_g):
    batch, seq, d = x.shape
    ctx_len = ctx.shape[1]
    depth = ada_down.shape[0]
    cfg = _Cfg(batch, seq, ctx_len, d)
    assert batch + 1 <= MOD_ROWS

    xs = jnp.concatenate([x.reshape(-1, d), ctx.reshape(-1, d)], axis=0)
    src = jnp.concatenate([c, c_ctx[None, :], jnp.zeros((MOD_ROWS - batch - 1, d), F32)], axis=0)
    mod = _ada_table(src, ada_down, ada_up, ada_b).reshape(depth * MOD_ROWS, 1, 6 * d)
    cos_t, sin_t = _rope_tables(seq)
    norm1_g3 = norm1_g.reshape(depth, 1, d)
    norm2_g3 = norm2_g.reshape(depth, 1, d)
    router_b3 = router_b.reshape(depth, -1, 1)
    b_subln_g3 = b_subln_g.reshape(b_subln_g.shape[0], 1, -1)
    sh_gate_bf, sh_up_bf, sh_down_bf = (w.astype(BF16) for w in (sh_gate, sh_up, sh_down))
    router_hi = router_w.astype(BF16)
    router_lo = (router_w - router_hi.astype(F32)).astype(BF16)
    router_hl = jnp.concatenate([router_hi, router_lo], axis=-1)
    exp_gate_t, exp_up_t = jnp.swapaxes(exp_gate, 2, 3), jnp.swapaxes(exp_up, 2, 3)

    a = _norm1(cfg, xs, norm1_g3, mod, 0)
    out = None
    for i in range(depth):
        kind, j = i % N_MIXERS, i // N_MIXERS
        need_ctx = i < depth - 1
        if kind == 0:
            nq = d
            nk = d // KV_GROUP
            qkv = _proj(cfg, a, a_w_qkv, j, rope=(cos_t, sin_t, nq + nk))
            o_lat, o_ctx = _gqa_attention(cfg, qkv, a_sink, j, need_ctx)
            w_o = a_w_o
        elif kind == 1:
            lam_init = 0.8 - 0.6 * math.exp(-0.3 * i)
            qkv = _proj(cfg, a, b_w_qkv, j, rope=(cos_t, sin_t, 2 * d))
            o_lat, o_ctx = _diff_attention(cfg, qkv, b_lam, b_subln_g3, j, lam_init, need_ctx)
            w_o = b_w_o
        else:
            pr = _proj(cfg, a, c_w_in, j)
            o = _conv_gate(cfg, pr, c_conv, j)
            w_o = c_w_out
        if kind != 2:
            o = o_lat if o_ctx is None else jnp.concatenate([o_lat, o_ctx], axis=0)
        elif not need_ctx:
            o = o[:cfg.t_lat]
        xs = _proj_resid(cfg, o, w_o, j, xs, mod, i, 2)

        fp, logits, hsh = _norm2(cfg, xs, norm2_g3, mod, i, router_hl, sh_gate_bf, sh_up_bf)
        eidx_t, wts_t, rank_t, counts = _route(logits, router_b3, i)
        pos, plan = _dispatch_plan(eidx_t, rank_t, counts, GMM_TILE)
        wts = wts_t.T
        xs_sorted = _dispatch(fp, pos)
        h_sorted = _gmm_up(xs_sorted, plan, exp_gate_t, exp_up_t, i)
        ys_sorted = _gmm_down(h_sorted, plan, exp_down, i)
        if i == depth - 1:
            out = _combine(cfg, xs, ys_sorted, pos, wts, hsh, sh_down_bf, mod, i,
                           final_g.reshape(1, d), True)
        else:
            xs, a = _combine(cfg, xs, ys_sorted, pos, wts, hsh, sh_down_bf, mod, i, norm1_g3, False)
    return out.reshape(batch, seq, d)
```
